```python
import math
import jax, jax.numpy as jnp
from jax import lax
import numpy as np

D_MODEL = 4096
BATCH = 4
SEQ = 2048
DEPTH = 2
DEC_BATCH = 8
DEC_SEQ = 1
PAST_LEN = 16384
PAGE_SIZE = 128

HEAD_DIM = 128
N_HEADS = 8
ATT_GROUPS = ((128, 1), (512, 4), (2048, 16))
N_GROUPS = len(ATT_GROUPS)
ATT_W = N_HEADS * HEAD_DIM
POOL_WINDOWS = (2, 4, 8, 16)
POOL_W = D_MODEL // 2
POOL_GC = POOL_W // len(POOL_WINDOWS)
POOL_CTX = max(POOL_WINDOWS) - 1
D_FF = 11008
CONV_W = 3
GATE_OFF = POOL_W + 3 * N_GROUPS * ATT_W
IN_W = GATE_OFF + 2 * D_MODEL
ALPHA = (2.0 * DEPTH) ** 0.25
BETA = (8.0 * DEPTH) ** -0.25
LN_EPS = 1e-5
NEG = -1e30

kernel_name = 'hybrid_pool_dilated_attn_convffn_step'


def _layer_norm(x, g, b):
    xf = x.astype(jnp.float32)
    mu = jnp.mean(xf, axis=-1, keepdims=True)
    var = jnp.mean(jnp.square(xf - mu), axis=-1, keepdims=True)
    y = (xf - mu) * lax.rsqrt(var + LN_EPS) * g.astype(jnp.float32) + b.astype(jnp.float32)
    return y.astype(x.dtype)


def _pool_mix(u_ext, pos0, w_grp, scale):
    B, L, P = u_ext.shape
    T = L - POOL_CTX
    uf = u_ext.astype(jnp.float32)
    c0 = jnp.concatenate([jnp.zeros((B, 1, P), jnp.float32), jnp.cumsum(uf, axis=1)], axis=1)
    n_seen = pos0 + jnp.arange(T, dtype=jnp.float32) + 1.0
    parts = []
    for g, w in enumerate(POOL_WINDOWS):
        c_lo, c_hi = g * POOL_GC, (g + 1) * POOL_GC
        hi = c0[:, POOL_CTX + 1:, c_lo:c_hi]
        lo = c0[:, POOL_CTX + 1 - w:POOL_CTX + 1 - w + T, c_lo:c_hi]
        cnt = jnp.minimum(n_seen, float(w))[None, :, None]
        parts.append((hi - lo) / cnt - uf[:, POOL_CTX:, c_lo:c_hi])
    d = jnp.stack(parts, axis=2)
    y = jnp.einsum('btgc,gce->btge', d, w_grp.astype(jnp.float32)).reshape(B, T, P)
    return (y * scale.astype(jnp.float32)).astype(u_ext.dtype)


def _dilated_band(q, k, v, dil, n):
    B, S, H, Dh = q.shape
    M = S // dil
    C = n
    nb = -(-M // C)
    Mp = nb * C

    def to_res(t):
        t = t.reshape(B, M, dil, H, Dh).transpose(0, 2, 1, 3, 4).reshape(B * dil, M, H, Dh)
        t = jnp.pad(t, ((0, 0), (0, Mp - M), (0, 0), (0, 0)))
        return t.reshape(B * dil, nb, C, H, Dh)

    def with_prev(t):
        prev = jnp.concatenate([jnp.zeros_like(t[:, :1]), t[:, :-1]], axis=1)
        return jnp.concatenate([prev, t], axis=2)

    qb = to_res(q)
    kk = with_prev(to_res(k))
    vv = with_prev(to_res(v))
    s = jnp.einsum('bnqhd,bnkhd->bnhqk', qb, kk).astype(jnp.float32) * (Dh ** -0.5)
    qi = jnp.arange(C)[:, None]
    kj = jnp.arange(2 * C)[None, :]
    dist = C + qi - kj
    blk = jnp.arange(nb)[:, None, None]
    valid = (dist >= 0) & (dist <= n) & ((blk - 1) * C + kj >= 0)
    s = jnp.where(valid[None, :, None], s, NEG)
    lse = jax.nn.logsumexp(s, axis=-1)
    p = jnp.exp(s - lse[..., None])
    o = jnp.einsum('bnhqk,bnkhd->bnqhd', p.astype(vv.dtype), vv)
    o = o.reshape(B, dil, Mp, H, Dh)[:, :, :M].transpose(0, 2, 1, 3, 4).reshape(B, S, H, Dh)
    lse = lse.transpose(0, 1, 3, 2).reshape(B, dil, Mp, H)[:, :, :M]
    lse = lse.transpose(0, 2, 1, 3).reshape(B, S, H)
    return o, lse


def _dilated_gather(q, kv_ext, dil, n):
    B, T, H, Dh = q.shape
    L = kv_ext.shape[1] - T
    idx = L + jnp.arange(T)[:, None] - dil * jnp.arange(n + 1)[None, :]
    valid = idx >= 0
    g = kv_ext[:, jnp.clip(idx, 0, None)]
    kg, vg = g[:, :, :, 0], g[:, :, :, 1]
    s = jnp.einsum('bqhd,bqkhd->bqhk', q, kg).astype(jnp.float32) * (Dh ** -0.5)
    s = jnp.where(valid[None, :, None, :], s, NEG)
    lse = jax.nn.logsumexp(s, axis=-1)
    p = jnp.exp(s - lse[..., None])
    o = jnp.einsum('bqhk,bqkhd->bqhd', p.astype(vg.dtype), vg)
    return o, lse


def _mixer(x, pool_prev, kv_prev, pos0, w_in, w_pool_grp, pool_scale, w_br_pool, w_br_att, w_out):
    B, T, _ = x.shape
    h = jnp.einsum('btd,de->bte', x, w_in)
    u_ext = jnp.concatenate([pool_prev, h[..., :POOL_W]], axis=1)
    y_pool = _pool_mix(u_ext, pos0, w_pool_grp, pool_scale)
    new_pool = u_ext[:, -POOL_CTX:]
    outs, lses, new_kv = [], [], []
    for gi, (win, dil) in enumerate(ATT_GROUPS):
        base = POOL_W + 3 * gi * ATT_W
        q = h[..., base:base + ATT_W].reshape(B, T, N_HEADS, HEAD_DIM)
        k = h[..., base + ATT_W:base + 2 * ATT_W].reshape(B, T, N_HEADS, HEAD_DIM)
        v = h[..., base + 2 * ATT_W:base + 3 * ATT_W].reshape(B, T, N_HEADS, HEAD_DIM)
        kv_new = jnp.stack([k, v], axis=2)
        n = win // dil
        if kv_prev is None:
            o, lse = _dilated_band(q, k, v, dil, n)
            kv_all = kv_new
        else:
            kv_all = jnp.concatenate([kv_prev[gi], kv_new], axis=1)
            o, lse = _dilated_gather(q, kv_all, dil, n)
        new_kv.append(kv_all[:, -min(win, pos0 + T):])
        outs.append(o.astype(jnp.float32))
        lses.append(lse)
    wg = jax.nn.softmax(jnp.stack(lses, axis=0), axis=0)
    o = jnp.sum(wg[..., None] * jnp.stack(outs, axis=0), axis=0)
    o = o.reshape(B, T, ATT_W).astype(x.dtype)
    g_pool = jax.nn.sigmoid(h[..., GATE_OFF:GATE_OFF + D_MODEL].astype(jnp.float32))
    g_att = jax.nn.sigmoid(h[..., GATE_OFF + D_MODEL:].astype(jnp.float32))
    merged = (g_pool * jnp.einsum('btp,pd->btd', y_pool, w_br_pool).astype(jnp.float32)
              + g_att * jnp.einsum('bta,ad->btd', o, w_br_att).astype(jnp.float32))
    out = jnp.einsum('btd,de->bte', merged.astype(x.dtype), w_out)
    return out, new_pool, new_kv


def _conv_ffn(x, conv_prev, w_up, conv_w, conv_b, w_down):
    T = x.shape[1]
    hu = jnp.einsum('btd,df->btf', x, w_up)
    a, b = hu[..., :D_FF], hu[..., D_FF:]
    a_ext = jnp.concatenate([conv_prev, a], axis=1)
    ac = (conv_w[0] * a_ext[:, 0:T] + conv_w[1] * a_ext[:, 1:T + 1]
          + conv_w[2] * a_ext[:, 2:T + 2] + conv_b)
    hm = jax.nn.gelu(ac.astype(jnp.float32), approximate=False) * b.astype(jnp.float32)
    y = jnp.einsum('btf,fd->btd', hm.astype(x.dtype), w_down)
    return y, a_ext[:, -(CONV_W - 1):]


def setup_inputs(seed: int = 0) -> dict:
    key = jax.random.key(seed)
    ks = jax.random.split(key, 24)
    f32 = jnp.float32

    def nrm(k, shape, scale):
        return jax.random.normal(k, shape, f32) * scale

    kvshape = lambda w: (DEPTH, DEC_BATCH, min(w, PAST_LEN), 2, N_HEADS, HEAD_DIM)
    return {
        'x_prompt': nrm(ks[0], (BATCH, SEQ, D_MODEL), 1.0),
        'x_sample': nrm(ks[1], (DEC_BATCH, DEC_SEQ, D_MODEL), 1.0),
        'state_pool': nrm(ks[2], (DEPTH, DEC_BATCH, POOL_CTX, POOL_W), 1.0),
        'cache_kv1': nrm(ks[3], kvshape(ATT_GROUPS[0][0]), 1.0),
        'cache_kv2': nrm(ks[4], kvshape(ATT_GROUPS[1][0]), 1.0),
        'cache_kv3': nrm(ks[5], kvshape(ATT_GROUPS[2][0]), 1.0),
        'state_conv': nrm(ks[6], (DEPTH, DEC_BATCH, CONV_W - 1, D_FF), 1.0),
        'w_in': nrm(ks[7], (DEPTH, D_MODEL, IN_W), D_MODEL ** -0.5),
        'w_pool_grp': nrm(ks[8], (DEPTH, len(POOL_WINDOWS), POOL_GC, POOL_GC), POOL_GC ** -0.5),
        'pool_scale': 1.0 + nrm(ks[9], (DEPTH, POOL_W), 0.1),
        'w_br_pool': nrm(ks[10], (DEPTH, POOL_W, D_MODEL), POOL_W ** -0.5),
        'w_br_att': nrm(ks[11], (DEPTH, ATT_W, D_MODEL), ATT_W ** -0.5),
        'w_out': nrm(ks[12], (DEPTH, D_MODEL, D_MODEL), BETA * D_MODEL ** -0.5),
        'ln1_g': 1.0 + nrm(ks[13], (DEPTH, D_MODEL), 0.05),
        'ln1_b': nrm(ks[14], (DEPTH, D_MODEL), 0.02),
        'w_up': nrm(ks[15], (DEPTH, D_MODEL, 2 * D_FF), D_MODEL ** -0.5),
        'conv_w': nrm(ks[16], (DEPTH, CONV_W, D_FF), CONV_W ** -0.5),
        'conv_b': nrm(ks[17], (DEPTH, D_FF), 0.02),
        'w_down': nrm(ks[18], (DEPTH, D_FF, D_MODEL), BETA * D_FF ** -0.5),
        'ln2_g': 1.0 + nrm(ks[19], (DEPTH, D_MODEL), 0.05),
        'ln2_b': nrm(ks[20], (DEPTH, D_MODEL), 0.02),
    }


def reference(x_prompt, x_sample, state_pool, cache_kv1, cache_kv2, cache_kv3, state_conv,
              w_in, w_pool_grp, pool_scale, w_br_pool, w_br_att, w_out, ln1_g, ln1_b,
              w_up, conv_w, conv_b, w_down, ln2_g, ln2_b):
    caches = (cache_kv1, cache_kv2, cache_kv3)

    def run(x, pos0, sample):
        Bn = x.shape[0]
        pools, convs = [], []
        kvs = [[] for _ in ATT_GROUPS]
        for l in range(DEPTH):
            if sample:
                pool_prev = state_pool[l].astype(x.dtype)
                kv_prev = [c[l].astype(x.dtype) for c in caches]
                conv_prev = state_conv[l].astype(x.dtype)
            else:
                pool_prev = jnp.zeros((Bn, POOL_CTX, POOL_W), x.dtype)
                kv_prev = None
                conv_prev = jnp.zeros((Bn, CONV_W - 1, D_FF), x.dtype)
            m, new_pool, new_kv = _mixer(x, pool_prev, kv_prev, pos0, w_in[l], w_pool_grp[l],
                                         pool_scale[l], w_br_pool[l], w_br_att[l], w_out[l])
            x = _layer_norm(ALPHA * x + m, ln1_g[l], ln1_b[l])
            f, new_conv = _conv_ffn(x, conv_prev, w_up[l], conv_w[l], conv_b[l], w_down[l])
            x = _layer_norm(ALPHA * x + f, ln2_g[l], ln2_b[l])
            pools.append(new_pool)
            convs.append(new_conv)
            for gi in range(N_GROUPS):
                kvs[gi].append(new_kv[gi])
        return (x, jnp.stack(pools), jnp.stack(kvs[0]), jnp.stack(kvs[1]),
                jnp.stack(kvs[2]), jnp.stack(convs))

    y_prompt, pool_p, kv1_p, kv2_p, kv3_p, conv_p = run(x_prompt, 0, False)
    y_sample, pool_s, kv1_s, kv2_s, kv3_s, conv_s = run(x_sample, PAST_LEN, True)
    return (y_prompt, y_sample, pool_p, pool_s, kv1_p, kv1_s, kv2_p, kv2_s, kv3_p, kv3_s, conv_p, conv_s)
```

```python
import functools

import jax
import jax.numpy as jnp
from jax import lax
from jax.experimental import pallas as pl
from jax.experimental.pallas import tpu as pltpu

D_MODEL = 4096
BATCH = 4
SEQ = 2048
DEPTH = 2
DEC_BATCH = 8
PAST_LEN = 16384
HEAD_DIM = 128
N_HEADS = 8
ATT_GROUPS = ((128, 1), (512, 4), (2048, 16))
N_GROUPS = len(ATT_GROUPS)
ATT_W = N_HEADS * HEAD_DIM
POOL_WINDOWS = (2, 4, 8, 16)
POOL_W = D_MODEL // 2
POOL_GC = POOL_W // len(POOL_WINDOWS)
POOL_CTX = max(POOL_WINDOWS) - 1
D_FF = 11008
GATE_OFF = POOL_W + 3 * N_GROUPS * ATT_W
IN_W = GATE_OFF + 2 * D_MODEL
ALPHA = (2.0 * DEPTH) ** 0.25
LN_EPS = 1e-5
NEG = -1e30
ATT_BLK = 128
SCALE = HEAD_DIM ** -0.5

DEC_ROWS = 16
HALO = 16
VMEM_LIMIT = 56 * 1024 * 1024

F32 = jnp.float32
BF16 = jnp.bfloat16


def _params(n_axes):
    return pltpu.CompilerParams(dimension_semantics=("arbitrary",) * n_axes,
                                vmem_limit_bytes=VMEM_LIMIT)


def _proj_kernel(*refs, alpha, has_resid):
    if has_resid:
        x_ref, w_ref, r_ref, o_ref, wb_ref = refs
    else:
        x_ref, w_ref, o_ref, wb_ref = refs

    @pl.when(pl.program_id(1) == 0)
    def _cast():
        wb_ref[...] = w_ref[...].astype(BF16)

    acc = jnp.dot(x_ref[...].astype(BF16), wb_ref[...], preferred_element_type=F32)
    if has_resid:
        acc = alpha * r_ref[...] + acc
    o_ref[...] = acc.astype(o_ref.dtype)


def _proj(x, w3, layer, *, n, tm, tn, resid=None, alpha=1.0, out_dtype=F32):
    m, k = x.shape
    grid = (n // tn, m // tm)
    in_specs = [pl.BlockSpec((tm, k), lambda j, i: (i, 0)),
                pl.BlockSpec((None, k, tn), lambda j, i: (layer, 0, j))]
    args = [x, w3]
    if resid is not None:
        in_specs.append(pl.BlockSpec((tm, tn), lambda j, i: (i, j)))
        args.append(resid)
    return pl.pallas_call(
        functools.partial(_proj_kernel, alpha=alpha, has_resid=resid is not None),
        grid=grid,
        in_specs=in_specs,
        out_specs=pl.BlockSpec((tm, tn), lambda j, i: (i, j)),
        out_shape=jax.ShapeDtypeStruct((m, n), out_dtype),
        scratch_shapes=[pltpu.VMEM((k, tn), BF16)],
        compiler_params=_params(2),
    )(*args)


def _down_kernel(x_ref, w_ref, r_ref, o_ref, acc_ref, *, alpha, tk, k_total):
    kk = pl.program_id(2)
    nk = pl.num_programs(2)

    @pl.when(kk == 0)
    def _init():
        acc_ref[...] = jnp.zeros_like(acc_ref)

    @pl.when(kk < nk - 1)
    def _full():
        acc_ref[...] += jnp.dot(x_ref[...], w_ref[...].astype(BF16), preferred_element_type=F32)

    @pl.when(kk == nk - 1)
    def _last():
        rem = k_total - (k_total // tk) * tk if k_total % tk else tk
        x = x_ref[...]
        w = w_ref[...]
        if rem != tk:
            col = lax.broadcasted_iota(jnp.int32, x.shape, 1)
            row = lax.broadcasted_iota(jnp.int32, w.shape, 0)
            x = jnp.where(col < rem, x, jnp.zeros_like(x))
            w = jnp.where(row < rem, w, jnp.zeros_like(w))
        acc = acc_ref[...] + jnp.dot(x, w.astype(BF16), preferred_element_type=F32)
        o_ref[...] = alpha * r_ref[...] + acc


def _down(x, w3, layer, resid, *, tm, tn, tk):
    m, k = x.shape
    n = w3.shape[2]
    grid = (n // tn, m // tm, pl.cdiv(k, tk))
    return pl.pallas_call(
        functools.partial(_down_kernel, alpha=ALPHA, tk=tk, k_total=k),
        grid=grid,
        in_specs=[pl.BlockSpec((tm, tk), lambda j, i, kk: (i, kk)),
                  pl.BlockSpec((None, tk, tn), lambda j, i, kk: (layer, kk, j)),
                  pl.BlockSpec((tm, tn), lambda j, i, kk: (i, j))],
        out_specs=pl.BlockSpec((tm, tn), lambda j, i, kk: (i, j)),
        out_shape=jax.ShapeDtypeStruct((m, n), F32),
        scratch_shapes=[pltpu.VMEM((tm, tn), F32)],
        compiler_params=_params(3),
    )(x, w3, resid)


def _ln_kernel(z_ref, g_ref, b_ref, o_ref, ob_ref):
    z = z_ref[...]
    mu = jnp.mean(z, axis=-1, keepdims=True)
    zc = z - mu
    var = jnp.mean(zc * zc, axis=-1, keepdims=True)
    y = zc * lax.rsqrt(var + LN_EPS) * g_ref[...] + b_ref[...]
    o_ref[...] = y
    ob_ref[...] = y.astype(BF16)


def _layer_norm(z, g2, b2, layer, *, tm):
    m, d = z.shape
    return pl.pallas_call(
        _ln_kernel,
        grid=(m // tm,),
        in_specs=[pl.BlockSpec((tm, d), lambda i: (i, 0)),
                  pl.BlockSpec((None, 1, d), lambda i: (layer, 0, 0)),
                  pl.BlockSpec((None, 1, d), lambda i: (layer, 0, 0))],
        out_specs=[pl.BlockSpec((tm, d), lambda i: (i, 0)),
                   pl.BlockSpec((tm, d), lambda i: (i, 0))],
        out_shape=[jax.ShapeDtypeStruct((m, d), F32), jax.ShapeDtypeStruct((m, d), BF16)],
        compiler_params=_params(1),
    )(z, g2, b2)


def _pool_groups(window_sum_fn, u_fn, inv_cnt_fn, wb_ref, sc_ref, y_ref):
    for g, w in enumerate(POOL_WINDOWS):
        cols = slice(g * POOL_GC, (g + 1) * POOL_GC)
        d = window_sum_fn(g, w, cols) * inv_cnt_fn(w) - u_fn(cols)
        y = jnp.dot(d.astype(BF16), wb_ref[g], preferred_element_type=F32) * sc_ref[:, cols]
        y_ref[:, cols] = y.astype(y_ref.dtype)


def _pool_seq_kernel(u_ref, halo_ref, wg_ref, sc_ref, y_ref, wb_ref, *, tm, tiles_per_seq):
    i = pl.program_id(0)

    @pl.when(i == 0)
    def _cast():
        wb_ref[...] = wg_ref[...].astype(BF16)

    t_in_seq = i % tiles_per_seq
    halo = jnp.where(t_in_seq == 0, 0.0, halo_ref[...])
    pos = t_in_seq * tm + lax.broadcasted_iota(jnp.int32, (tm, 1), 0)

    def window_sum(g, w, cols):
        s = jnp.concatenate([halo[:, cols], u_ref[:, cols]], axis=0)
        k = 1
        while k < w:
            s = s + pltpu.roll(s, k, 0)
            k *= 2
        return s[HALO:]

    def inv_cnt(w):
        return 1.0 / jnp.minimum(pos + 1, w).astype(F32)

    _pool_groups(window_sum, lambda cols: u_ref[:, cols], inv_cnt, wb_ref, sc_ref, y_ref)


def _pool_seq(h, w_pool_grp, pool_scale, layer, *, tm):
    m = h.shape[0]
    tiles_per_seq = SEQ // tm
    return pl.pallas_call(
        functools.partial(_pool_seq_kernel, tm=tm, tiles_per_seq=tiles_per_seq),
        grid=(m // tm,),
        in_specs=[pl.BlockSpec((tm, POOL_W), lambda i: (i, 0)),
                  pl.BlockSpec((HALO, POOL_W), lambda i: (jnp.maximum(i * (tm // HALO) - 1, 0), 0)),
                  pl.BlockSpec((None, len(POOL_WINDOWS), POOL_GC, POOL_GC), lambda i: (layer, 0, 0, 0)),
                  pl.BlockSpec((None, 1, POOL_W), lambda i: (layer, 0, 0))],
        out_specs=pl.BlockSpec((tm, POOL_W), lambda i: (i, 0)),
        out_shape=jax.ShapeDtypeStruct((m, POOL_W), BF16),
        scratch_shapes=[pltpu.VMEM((len(POOL_WINDOWS), POOL_GC, POOL_GC), BF16)],
        compiler_params=_params(1),
    )(h, h, w_pool_grp, pool_scale)


def _pool_dec_kernel(u_ref, st_ref, wg_ref, sc_ref, y_ref, wb_ref):
    wb_ref[...] = wg_ref[...].astype(BF16)

    def window_sum(g, w, cols):
        s = u_ref[:, cols]
        for k in range(1, w):
            s = s + st_ref[POOL_CTX - k, :, cols]
        return s

    def inv_cnt(w):
        return 1.0 / float(min(PAST_LEN + 1, w))

    _pool_groups(window_sum, lambda cols: u_ref[:, cols], inv_cnt, wb_ref, sc_ref, y_ref)


def _pool_dec(h, st, w_pool_grp, pool_scale, layer):
    m = h.shape[0]
    return pl.pallas_call(
        _pool_dec_kernel,
        grid=(1,),
        in_specs=[pl.BlockSpec((m, POOL_W), lambda i: (0, 0)),
                  pl.BlockSpec((POOL_CTX, m, POOL_W), lambda i: (0, 0, 0)),
                  pl.BlockSpec((None, len(POOL_WINDOWS), POOL_GC, POOL_GC), lambda i: (layer, 0, 0, 0)),
                  pl.BlockSpec((None, 1, POOL_W), lambda i: (layer, 0, 0))],
        out_specs=pl.BlockSpec((m, POOL_W), lambda i: (0, 0)),
        out_shape=jax.ShapeDtypeStruct((m, POOL_W), BF16),
        scratch_shapes=[pltpu.VMEM((len(POOL_WINDOWS), POOL_GC, POOL_GC), BF16)],
        compiler_params=_params(1),
    )(h, st, w_pool_grp, pool_scale)


def _merge_groups(o_list, lse_list):
    mx = functools.reduce(jnp.maximum, lse_list)
    ws = [jnp.exp(l - mx) for l in lse_list]
    num = functools.reduce(lambda a, b: a + b, [w * o for w, o in zip(ws, o_list)])
    den = functools.reduce(lambda a, b: a + b, ws)
    return num / den


def _attn_seq_kernel(*refs):
    qkv = refs[:9]
    o_ref = refs[9]
    og_refs = refs[10:13]
    lg_refs = refs[13:16]

    qi = lax.broadcasted_iota(jnp.int32, (ATT_BLK, ATT_BLK), 0)
    kj = lax.broadcasted_iota(jnp.int32, (ATT_BLK, ATT_BLK), 1)
    cur_mask = kj <= qi
    qi2 = lax.broadcasted_iota(jnp.int32, (ATT_BLK, 2 * ATT_BLK), 0)
    kj2 = lax.broadcasted_iota(jnp.int32, (ATT_BLK, 2 * ATT_BLK), 1)
    both_mask = (kj2 >= qi2) & (kj2 - qi2 <= ATT_BLK)

    for g, (win, dil) in enumerate(ATT_GROUPS):
        q_ref, k_ref, v_ref = qkv[3 * g:3 * g + 3]
        n_blk = SEQ // dil // ATT_BLK
        for r in range(dil):
            for nb in range(n_blk):
                q_rows = pl.ds(r + nb * ATT_BLK * dil, ATT_BLK, stride=dil)
                if nb == 0:
                    kv_rows, mask = q_rows, cur_mask
                else:
                    kv_rows = pl.ds(r + (nb - 1) * ATT_BLK * dil, 2 * ATT_BLK, stride=dil)
                    mask = both_mask
                q = q_ref[q_rows, :].astype(BF16)
                k = k_ref[kv_rows, :].astype(BF16)
                v = v_ref[kv_rows, :].astype(BF16)
                s = lax.dot_general(q, k, (((1,), (1,)), ((), ())), preferred_element_type=F32) * SCALE
                s = jnp.where(mask, s, NEG)
                mx = jnp.max(s, axis=-1, keepdims=True)
                p = jnp.exp(s - mx)
                den = jnp.sum(p, axis=-1, keepdims=True)
                o = jnp.dot(p.astype(BF16), v, preferred_element_type=F32) / den
                og_refs[g][q_rows, :] = o
                lg_refs[g][q_rows, :] = jnp.broadcast_to(mx + jnp.log(den), (ATT_BLK, HEAD_DIM))

    chunk = 256
    for c in range(SEQ // chunk):
        rows = pl.ds(c * chunk, chunk)
        out = _merge_groups([r_[rows, :] for r_ in og_refs], [r_[rows, :] for r_ in lg_refs])
        o_ref[rows, :] = out.astype(o_ref.dtype)


def _attn_seq(h):
    m = h.shape[0]
    in_specs = []
    for g in range(N_GROUPS):
        for which in range(3):
            blk0 = (POOL_W + (3 * g + which) * ATT_W) // HEAD_DIM
            in_specs.append(pl.BlockSpec((SEQ, HEAD_DIM), lambda b, hd, blk0=blk0: (b, blk0 + hd)))
    return pl.pallas_call(
        _attn_seq_kernel,
        grid=(m // SEQ, N_HEADS),
        in_specs=in_specs,
        out_specs=pl.BlockSpec((SEQ, HEAD_DIM), lambda b, hd: (b, hd)),
        out_shape=jax.ShapeDtypeStruct((m, ATT_W), BF16),
        scratch_shapes=[pltpu.VMEM((SEQ, HEAD_DIM), F32)] * (2 * N_GROUPS),
        compiler_params=_params(2),
    )(*([h] * 9))


def _attn_dec_kernel(h_ref, c1_ref, c2_ref, c3_ref, o_ref):
    def rnd(t):
        return t.astype(BF16).astype(F32)

    caches = (c1_ref, c2_ref, c3_ref)
    per_group = []
    for g in range(N_GROUPS):
        base = POOL_W + 3 * g * ATT_W
        q = rnd(h_ref[:, base:base + ATT_W])
        k_new = rnd(h_ref[:, base + ATT_W:base + 2 * ATT_W])
        v_new = rnd(h_ref[:, base + 2 * ATT_W:base + 3 * ATT_W])
        qk = rnd(caches[g][0, :, :ATT_W]) * q
        qk_new = k_new * q
        vc = rnd(caches[g][0, :, ATT_W:])
        outs, lses = [], []
        for hd in range(N_HEADS):
            lanes = slice(hd * HEAD_DIM, (hd + 1) * HEAD_DIM)
            s = jnp.sum(qk[:, lanes], axis=-1, keepdims=True) * SCALE
            s_new = jnp.sum(qk_new[:, lanes], axis=-1, keepdims=True) * SCALE
            mx = jnp.maximum(jnp.max(s, axis=0, keepdims=True), s_new)
            p = jnp.exp(s - mx)
            p_new = jnp.exp(s_new - mx)
            den = jnp.sum(p, axis=0, keepdims=True) + p_new
            acc = jnp.sum(rnd(p) * vc[:, lanes], axis=0, keepdims=True) + rnd(p_new) * v_new[:, lanes]
            outs.append(acc / den)
            lses.append(mx + jnp.log(den))
        per_group.append((outs, lses))
    for hd in range(N_HEADS):
        out = _merge_groups([per_group[g][0][hd] for g in range(N_GROUPS)],
                            [per_group[g][1][hd] for g in range(N_GROUPS)])
        o_ref[:, hd * HEAD_DIM:(hd + 1) * HEAD_DIM] = out


def _attn_dec(h, caches, layer):
    m = h.shape[0]
    in_specs = [pl.BlockSpec((None, 1, IN_W), lambda b: (b, 0, 0))]
    for g, (win, dil) in enumerate(ATT_GROUPS):
        in_specs.append(pl.BlockSpec((1, win // dil, 2 * ATT_W),
                                     lambda b: (layer * DEC_BATCH + b, 0, 0)))
    o = pl.pallas_call(
        _attn_dec_kernel,
        grid=(DEC_BATCH,),
        in_specs=in_specs,
        out_specs=pl.BlockSpec((None, 1, ATT_W), lambda b: (b, 0, 0)),
        out_shape=jax.ShapeDtypeStruct((DEC_BATCH, 1, ATT_W), F32),
        compiler_params=_params(1),
    )(h.reshape(m, 1, IN_W), *caches)
    return jnp.pad(o.reshape(DEC_BATCH, ATT_W), ((0, m - DEC_BATCH), (0, 0)))


def _gate_kernel(yp_ref, oa_ref, wp_ref, wa_ref, gp_ref, ga_ref, o_ref, wpb_ref, wab_ref):
    @pl.when(pl.program_id(1) == 0)
    def _cast():
        wpb_ref[...] = wp_ref[...].astype(BF16)
        wab_ref[...] = wa_ref[...].astype(BF16)

    bp = jnp.dot(yp_ref[...].astype(BF16), wpb_ref[...], preferred_element_type=F32)
    ba = jnp.dot(oa_ref[...].astype(BF16), wab_ref[...], preferred_element_type=F32)
    merged = jax.nn.sigmoid(gp_ref[...]) * bp + jax.nn.sigmoid(ga_ref[...]) * ba
    o_ref[...] = merged.astype(o_ref.dtype)


def _gate_merge(y_pool, o_att, h, w_br_pool, w_br_att, layer, *, tm, tn):
    m = h.shape[0]
    gp0 = GATE_OFF // tn
    ga0 = (GATE_OFF + D_MODEL) // tn
    return pl.pallas_call(
        _gate_kernel,
        grid=(D_MODEL // tn, m // tm),
        in_specs=[pl.BlockSpec((tm, POOL_W), lambda j, i: (i, 0)),
                  pl.BlockSpec((tm, ATT_W), lambda j, i: (i, 0)),
                  pl.BlockSpec((None, POOL_W, tn), lambda j, i: (layer, 0, j)),
                  pl.BlockSpec((None, ATT_W, tn), lambda j, i: (layer, 0, j)),
                  pl.BlockSpec((tm, tn), lambda j, i: (i, gp0 + j)),
                  pl.BlockSpec((tm, tn), lambda j, i: (i, ga0 + j))],
        out_specs=pl.BlockSpec((tm, tn), lambda j, i: (i, j)),
        out_shape=jax.ShapeDtypeStruct((m, D_MODEL), BF16),
        scratch_shapes=[pltpu.VMEM((POOL_W, tn), BF16), pltpu.VMEM((ATT_W, tn), BF16)],
        compiler_params=_params(2),
    )(y_pool, o_att, w_br_pool, w_br_att, h, h)


def _ffn_up_kernel(*refs, tm, tiles_per_seq, sequential):
    if sequential:
        x_ref, wa_ref, wb_ref, cw_ref, cb_ref, hm_ref, tail_ref, wab_ref, wbb_ref, carry_ref = refs
    else:
        x_ref, wa_ref, wb_ref, cw_ref, cb_ref, p2_ref, p1_ref, hm_ref, tail_ref, wab_ref, wbb_ref = refs
    i = pl.program_id(1)

    @pl.when(i == 0)
    def _cast():
        wab_ref[...] = wa_ref[...].astype(BF16)
        wbb_ref[...] = wb_ref[...].astype(BF16)

    x = x_ref[...]
    a = jnp.dot(x, wab_ref[...], preferred_element_type=F32)
    b = jnp.dot(x, wbb_ref[...], preferred_element_type=F32)
    if sequential:
        @pl.when(i % tiles_per_seq == 0)
        def _reset():
            carry_ref[...] = jnp.zeros_like(carry_ref)

        carry = carry_ref[...]
        row = lax.broadcasted_iota(jnp.int32, (tm, 1), 0)
        a1 = jnp.where(row == 0, carry[7:8], pltpu.roll(a, 1, 0))
        a2 = jnp.where(row == 0, carry[6:7], jnp.where(row == 1, carry[7:8], pltpu.roll(a, 2, 0)))
        carry_ref[...] = a[tm - 8:]
        tail_ref[0] = a[tm - 8:]
    else:
        a2, a1 = p2_ref[...], p1_ref[...]
        tail_ref[...] = a
    ac = cw_ref[0:1] * a2 + cw_ref[1:2] * a1 + cw_ref[2:3] * a + cb_ref[...]
    gelu = 0.5 * ac * (1.0 + lax.erf(ac * (0.5 ** 0.5)))
    hm_ref[...] = (gelu * b).astype(hm_ref.dtype)


def _ffn_up(x, w_up, conv_w, conv_b, layer, *, tm, tn, prev=None):
    m, k = x.shape
    n_panels = D_FF // tn
    sequential = prev is None
    in_specs = [pl.BlockSpec((tm, k), lambda j, i: (i, 0)),
                pl.BlockSpec((None, k, tn), lambda j, i: (layer, 0, j)),
                pl.BlockSpec((None, k, tn), lambda j, i: (layer, 0, n_panels + j)),
                pl.BlockSpec((None, 3, tn), lambda j, i: (layer, 0, j)),
                pl.BlockSpec((None, 1, tn), lambda j, i: (layer, 0, j))]
    args = [x, w_up, w_up, conv_w, conv_b]
    scratch = [pltpu.VMEM((k, tn), BF16), pltpu.VMEM((k, tn), BF16)]
    if sequential:
        tail_shape = jax.ShapeDtypeStruct((m // tm, 8, D_FF), F32)
        tail_spec = pl.BlockSpec((1, 8, tn), lambda j, i: (i, 0, j))
        scratch.append(pltpu.VMEM((8, tn), F32))
    else:
        in_specs += [pl.BlockSpec((tm, tn), lambda j, i: (i, j))] * 2
        args += list(prev)
        tail_shape = jax.ShapeDtypeStruct((m, D_FF), F32)
        tail_spec = pl.BlockSpec((tm, tn), lambda j, i: (i, j))
    return pl.pallas_call(
        functools.partial(_ffn_up_kernel, tm=tm, tiles_per_seq=max(SEQ // tm, 1), sequential=sequential),
        grid=(n_panels, m // tm),
        in_specs=in_specs,
        out_specs=[pl.BlockSpec((tm, tn), lambda j, i: (i, j)), tail_spec],
        out_shape=[jax.ShapeDtypeStruct((m, D_FF), BF16), tail_shape],
        scratch_shapes=scratch,
        compiler_params=_params(2),
    )(*args)


def _layer(x, xb, layer, p, *, tm, dec=None):
    h = _proj(xb, p['w_in'], layer, n=IN_W, tm=tm, tn=512)
    if dec is None:
        y_pool = _pool_seq(h, p['w_pool_grp'], p['pool_scale'], layer, tm=512)
        o_att = _attn_seq(h)
    else:
        y_pool = _pool_dec(h, dec['pool'][layer], p['w_pool_grp'], p['pool_scale'], layer)
        o_att = _attn_dec(h, dec['caches'], layer)
    merged = _gate_merge(y_pool, o_att, h, p['w_br_pool'], p['w_br_att'], layer, tm=tm, tn=512)
    z = _proj(merged, p['w_out'], layer, n=D_MODEL, tm=tm, tn=512, resid=x, alpha=ALPHA)
    x1, x1b = _layer_norm(z, p['ln1_g'], p['ln1_b'], layer, tm=min(256, z.shape[0]))
    prev = None if dec is None else (dec['conv'][layer, 0], dec['conv'][layer, 1])
    hm, tail = _ffn_up(x1b, p['w_up'], p['conv_w'], p['conv_b'], layer, tm=tm, tn=256, prev=prev)
    z2 = _down(hm, p['w_down'], layer, x1, tm=tm, tn=1024, tk=1024)
    x2, x2b = _layer_norm(z2, p['ln2_g'], p['ln2_b'], layer, tm=min(256, z2.shape[0]))
    return x2, x2b, h, tail


def kernel(x_prompt, x_sample, state_pool, cache_kv1, cache_kv2, cache_kv3, state_conv, w_in, w_pool_grp, pool_scale, w_br_pool, w_br_att, w_out, ln1_g, ln1_b, w_up, conv_w, conv_b, w_down, ln2_g, ln2_b):
    def row3(t):
        return t.reshape(DEPTH, 1, t.shape[-1])

    p = dict(w_in=w_in, w_pool_grp=w_pool_grp, pool_scale=row3(pool_scale), w_br_pool=w_br_pool,
             w_br_att=w_br_att, w_out=w_out, ln1_g=row3(ln1_g), ln1_b=row3(ln1_b), w_up=w_up,
             conv_w=conv_w, conv_b=row3(conv_b), w_down=w_down, ln2_g=row3(ln2_g), ln2_b=row3(ln2_b))
    caches_in = (cache_kv1, cache_kv2, cache_kv3)

    x = x_prompt.reshape(BATCH * SEQ, D_MODEL)
    xb = x.astype(BF16)
    pools_p, convs_p = [], []
    kvs_p = [[] for _ in ATT_GROUPS]
    for l in range(DEPTH):
        x, xb, h, tail = _layer(x, xb, l, p, tm=1024)
        h3 = h.reshape(BATCH, SEQ, IN_W)
        pools_p.append(h3[:, SEQ - POOL_CTX:, :POOL_W])
        for g, (win, _) in enumerate(ATT_GROUPS):
            base = POOL_W + 3 * g * ATT_W + ATT_W
            keep = min(win, SEQ)
            kvs_p[g].append(h3[:, SEQ - keep:, base:base + 2 * ATT_W].reshape(BATCH, keep, 2, N_HEADS, HEAD_DIM))
        tiles_per_seq = SEQ // 1024
        t4 = tail.reshape(BATCH, tiles_per_seq, 8, D_FF)
        convs_p.append(t4[:, tiles_per_seq - 1, 6:8, :])
    y_prompt = x.reshape(BATCH, SEQ, D_MODEL)

    pad = DEC_ROWS - DEC_BATCH
    xs = jnp.pad(x_sample.reshape(DEC_BATCH, D_MODEL), ((0, pad), (0, 0)))
    dec = dict(
        pool=jnp.pad(jnp.transpose(state_pool, (0, 2, 1, 3)), ((0, 0), (0, 0), (0, pad), (0, 0))),
        conv=jnp.pad(jnp.transpose(state_conv, (0, 2, 1, 3)), ((0, 0), (0, 0), (0, pad), (0, 0))),
        caches=[c.reshape(DEPTH * DEC_BATCH, c.shape[2] // dil, dil * 2 * ATT_W)
                for c, (_, dil) in zip(caches_in, ATT_GROUPS)],
    )
    xsb = xs.astype(BF16)
    pools_s, convs_s = [], []
    kvs_s = [[] for _ in ATT_GROUPS]
    for l in range(DEPTH):
        xs, xsb, h, a_new = _layer(xs, xsb, l, p, tm=DEC_ROWS, dec=dec)
        h = h[:DEC_BATCH]
        pools_s.append(jnp.concatenate([state_pool[l][:, 1:], h[:, None, :POOL_W]], axis=1))
        for g, (win, _) in enumerate(ATT_GROUPS):
            base = POOL_W + 3 * g * ATT_W + ATT_W
            kv_new = h[:, base:base + 2 * ATT_W].reshape(DEC_BATCH, 1, 2, N_HEADS, HEAD_DIM)
            kv_all = jnp.concatenate([caches_in[g][l], kv_new], axis=1)
            kvs_s[g].append(kv_all[:, -min(win, PAST_LEN + 1):])
        convs_s.append(jnp.concatenate([state_conv[l][:, 1:], a_new[:DEC_BATCH, None, :]], axis=1))
    y_sample = xs[:DEC_BATCH].reshape(DEC_BATCH, 1, D_MODEL)

    st = jnp.stack
    return (y_prompt, y_sample, st(pools_p), st(pools_s), st(kvs_p[0]), st(kvs_s[0]), st(kvs_p[1]),
            st(kvs_s[1]), st(kvs_p[2]), st(kvs_s[2]), st(convs_p), st(convs_s))
```

```python
import functools

import jax
import jax.numpy as jnp
from jax import lax
from jax.experimental import pallas as pl
from jax.experimental.pallas import tpu as pltpu

D_MODEL = 4096
SEQ = 2048
DEPTH = 2
DEC_BATCH = 8
PAST_LEN = 16384
HEAD_DIM = 128
N_HEADS = 8
ATT_GROUPS = ((128, 1), (512, 4), (2048, 16))
N_GROUPS = len(ATT_GROUPS)
ATT_W = N_HEADS * HEAD_DIM
POOL_WINDOWS = (2, 4, 8, 16)
POOL_W = D_MODEL // 2
POOL_GC = POOL_W // len(POOL_WINDOWS)
POOL_CTX = max(POOL_WINDOWS) - 1
D_FF = 11008
GATE_OFF = POOL_W + 3 * N_GROUPS * ATT_W
IN_W = GATE_OFF + 2 * D_MODEL
ALPHA = (2.0 * DEPTH) ** 0.25
LN_EPS = 1e-5
NEG = -1e30
ATT_BLK = 128
SCALE = HEAD_DIM ** -0.5

DEC_ROWS = 16
HALO = 16
VMEM_LIMIT = 62 * 1024 * 1024

F32 = jnp.float32
BF16 = jnp.bfloat16


def _tiles(m):
    big = m >= 1024
    return dict(
        proj=(512 if big else m, 1024),
        gate=(512 if big else m, 1024),
        ffn_up=(1024 if big else m, 256),
        ffn_chunks=2 if big else 1,
        down=(1024 if big else m, 1024, 2048),
        ln=256 if big else m,
        pool=512,
    )


def _params(n_axes):
    return pltpu.CompilerParams(dimension_semantics=("arbitrary",) * n_axes,
                                vmem_limit_bytes=VMEM_LIMIT)


def _proj_kernel(*refs, alpha, has_resid, has_alias):
    refs = list(refs)
    x_ref, w_ref = refs[:2]
    r_ref = refs[2] if has_resid else None
    o_ref, wb_ref = refs[-2:]

    @pl.when(pl.program_id(1) == 0)
    def _cast():
        wb_ref[...] = w_ref[...].astype(BF16)

    acc = jnp.dot(x_ref[...], wb_ref[...], preferred_element_type=F32)
    if has_resid:
        acc = alpha * r_ref[...] + acc
    o_ref[...] = acc.astype(o_ref.dtype)


def _proj(x, w3, layer, *, n, tm, tn, resid=None, alpha=1.0, stack=None, name):
    m, k = x.shape
    grid = (n // tn, m // tm)
    in_specs = [pl.BlockSpec((tm, k), lambda j, i: (i, 0)),
                pl.BlockSpec((None, k, tn), lambda j, i: (layer, 0, j))]
    args = [x, w3]
    if resid is not None:
        in_specs.append(pl.BlockSpec((tm, tn), lambda j, i: (i, j)))
        args.append(resid)
    aliases = {}
    if stack is None:
        out_shape = jax.ShapeDtypeStruct((m, n), F32)
        out_spec = pl.BlockSpec((tm, tn), lambda j, i: (i, j))
    else:
        depth, prev = stack
        out_shape = jax.ShapeDtypeStruct((depth, m, n), F32)
        out_spec = pl.BlockSpec((None, tm, tn), lambda j, i: (layer, i, j))
        if prev is not None:
            in_specs.append(pl.BlockSpec(memory_space=pl.ANY))
            aliases = {len(args): 0}
            args.append(prev)
    return pl.pallas_call(
        functools.partial(_proj_kernel, alpha=alpha, has_resid=resid is not None,
                          has_alias=bool(aliases)),
        grid=grid,
        in_specs=in_specs,
        out_specs=out_spec,
        out_shape=out_shape,
        scratch_shapes=[pltpu.VMEM((k, tn), BF16)],
        input_output_aliases=aliases,
        compiler_params=_params(2),
        name=name,
    )(*args)


def _down_kernel(x_ref, w_ref, r_ref, o_ref, acc_ref, *, alpha, rem):
    kk = pl.program_id(2)
    nk = pl.num_programs(2)

    @pl.when(kk == 0)
    def _init():
        acc_ref[...] = jnp.zeros_like(acc_ref)

    @pl.when(kk < nk - 1)
    def _full():
        acc_ref[...] += jnp.dot(x_ref[...], w_ref[...].astype(BF16), preferred_element_type=F32)

    @pl.when(kk == nk - 1)
    def _last():
        part = jnp.dot(x_ref[:, :rem], w_ref[:rem, :].astype(BF16), preferred_element_type=F32)
        o_ref[...] = alpha * r_ref[...] + (acc_ref[...] + part)


def _down(x, w3, layer, resid, *, tm, tn, tk):
    m, k = x.shape
    n = w3.shape[2]
    nk = pl.cdiv(k, tk)
    rem = k - (nk - 1) * tk
    return pl.pallas_call(
        functools.partial(_down_kernel, alpha=ALPHA, rem=rem),
        grid=(n // tn, m // tm, nk),
        in_specs=[pl.BlockSpec((tm, tk), lambda j, i, kk: (i, kk)),
                  pl.BlockSpec((None, tk, tn), lambda j, i, kk: (layer, kk, j)),
                  pl.BlockSpec((tm, tn), lambda j, i, kk: (i, j))],
        out_specs=pl.BlockSpec((tm, tn), lambda j, i, kk: (i, j)),
        out_shape=jax.ShapeDtypeStruct((m, n), F32),
        scratch_shapes=[pltpu.VMEM((tm, tn), F32)],
        compiler_params=_params(3),
        name="down_proj",
    )(x, w3, resid)


def _ln_kernel(z_ref, g_ref, b_ref, o_ref, ob_ref):
    z = z_ref[...]
    mu = jnp.mean(z, axis=-1, keepdims=True)
    zc = z - mu
    var = jnp.mean(zc * zc, axis=-1, keepdims=True)
    y = zc * lax.rsqrt(var + LN_EPS) * g_ref[...] + b_ref[...]
    o_ref[...] = y
    ob_ref[...] = y.astype(BF16)


def _layer_norm(z, g3, b3, layer, *, tm):
    m, d = z.shape
    return pl.pallas_call(
        _ln_kernel,
        grid=(m // tm,),
        in_specs=[pl.BlockSpec((tm, d), lambda i: (i, 0)),
                  pl.BlockSpec((None, 1, d), lambda i: (layer, 0, 0)),
                  pl.BlockSpec((None, 1, d), lambda i: (layer, 0, 0))],
        out_specs=[pl.BlockSpec((tm, d), lambda i: (i, 0)),
                   pl.BlockSpec((tm, d), lambda i: (i, 0))],
        out_shape=[jax.ShapeDtypeStruct((m, d), F32), jax.ShapeDtypeStruct((m, d), BF16)],
        compiler_params=_params(1),
        name="layer_norm",
    )(z, g3, b3)


def _pool_groups(window_sum_fn, u_fn, inv_cnt_fn, wb_ref, sc_ref, y_ref):
    for g, w in enumerate(POOL_WINDOWS):
        cols = slice(g * POOL_GC, (g + 1) * POOL_GC)
        d = window_sum_fn(g, w, cols) * inv_cnt_fn(w) - u_fn(cols)
        y = jnp.dot(d.astype(BF16), wb_ref[g], preferred_element_type=F32) * sc_ref[:, cols]
        y_ref[:, cols] = y.astype(y_ref.dtype)


def _pool_seq_kernel(u_ref, halo_ref, wg_ref, sc_ref, y_ref, wb_ref, *, tm, tiles_per_seq):
    i = pl.program_id(0)

    @pl.when(i == 0)
    def _cast():
        wb_ref[...] = wg_ref[...].astype(BF16)

    t_in_seq = i % tiles_per_seq
    halo = jnp.where(t_in_seq == 0, 0.0, halo_ref[...])
    pos = t_in_seq * tm + lax.broadcasted_iota(jnp.int32, (tm, 1), 0)

    def window_sum(g, w, cols):
        s = jnp.concatenate([halo[:, cols], u_ref[:, cols]], axis=0)
        k = 1
        while k < w:
            s = s + pltpu.roll(s, k, 0)
            k *= 2
        return s[HALO:]

    def inv_cnt(w):
        return 1.0 / jnp.minimum(pos + 1, w).astype(F32)

    _pool_groups(window_sum, lambda cols: u_ref[:, cols], inv_cnt, wb_ref, sc_ref, y_ref)


def _pool_seq(h, w_pool_grp, pool_scale, layer, *, tm):
    m = h.shape[1]
    tiles_per_seq = SEQ // tm
    return pl.pallas_call(
        functools.partial(_pool_seq_kernel, tm=tm, tiles_per_seq=tiles_per_seq),
        grid=(m // tm,),
        in_specs=[pl.BlockSpec((None, tm, POOL_W), lambda i: (layer, i, 0)),
                  pl.BlockSpec((None, HALO, POOL_W),
                               lambda i: (layer, jnp.maximum(i * (tm // HALO) - 1, 0), 0)),
                  pl.BlockSpec((None, len(POOL_WINDOWS), POOL_GC, POOL_GC), lambda i: (layer, 0, 0, 0)),
                  pl.BlockSpec((None, 1, POOL_W), lambda i: (layer, 0, 0))],
        out_specs=pl.BlockSpec((tm, POOL_W), lambda i: (i, 0)),
        out_shape=jax.ShapeDtypeStruct((m, POOL_W), BF16),
        scratch_shapes=[pltpu.VMEM((len(POOL_WINDOWS), POOL_GC, POOL_GC), BF16)],
        compiler_params=_params(1),
        name="pool_seq",
    )(h, h, w_pool_grp, pool_scale)


def _pool_dec_kernel(u_ref, st_ref, wg_ref, sc_ref, y_ref, wb_ref):
    wb_ref[...] = wg_ref[...].astype(BF16)

    def window_sum(g, w, cols):
        s = u_ref[:, cols]
        for k in range(1, w):
            s = s + st_ref[POOL_CTX - k, :, cols]
        return s

    def inv_cnt(w):
        return 1.0 / float(min(PAST_LEN + 1, w))

    _pool_groups(window_sum, lambda cols: u_ref[:, cols], inv_cnt, wb_ref, sc_ref, y_ref)


def _pool_dec(h, st, w_pool_grp, pool_scale, layer):
    m = h.shape[1]
    return pl.pallas_call(
        _pool_dec_kernel,
        grid=(1,),
        in_specs=[pl.BlockSpec((None, m, POOL_W), lambda i: (layer, 0, 0)),
                  pl.BlockSpec((None, POOL_CTX, m, POOL_W), lambda i: (layer, 0, 0, 0)),
                  pl.BlockSpec((None, len(POOL_WINDOWS), POOL_GC, POOL_GC), lambda i: (layer, 0, 0, 0)),
                  pl.BlockSpec((None, 1, POOL_W), lambda i: (layer, 0, 0))],
        out_specs=pl.BlockSpec((m, POOL_W), lambda i: (0, 0)),
        out_shape=jax.ShapeDtypeStruct((m, POOL_W), BF16),
        scratch_shapes=[pltpu.VMEM((len(POOL_WINDOWS), POOL_GC, POOL_GC), BF16)],
        compiler_params=_params(1),
        name="pool_dec",
    )(h, st, w_pool_grp, pool_scale)


def _merge_groups(o_list, lse_list):
    mx = functools.reduce(jnp.maximum, lse_list)
    ws = [jnp.exp(l - mx) for l in lse_list]
    num = functools.reduce(lambda a, b: a + b, [w * o for w, o in zip(ws, o_list)])
    den = functools.reduce(lambda a, b: a + b, ws)
    return num / den


def _attn_seq_kernel(*refs):
    qkv = refs[:9]
    o_ref = refs[9]
    og_refs = refs[10:13]
    lg_refs = refs[13:16]

    qi = lax.broadcasted_iota(jnp.int32, (ATT_BLK, ATT_BLK), 0)
    kj = lax.broadcasted_iota(jnp.int32, (ATT_BLK, ATT_BLK), 1)
    cur_mask = kj <= qi
    qi2 = lax.broadcasted_iota(jnp.int32, (ATT_BLK, 2 * ATT_BLK), 0)
    kj2 = lax.broadcasted_iota(jnp.int32, (ATT_BLK, 2 * ATT_BLK), 1)
    both_mask = (kj2 >= qi2) & (kj2 - qi2 <= ATT_BLK)

    for g, (win, dil) in enumerate(ATT_GROUPS):
        q_ref, k_ref, v_ref = qkv[3 * g:3 * g + 3]
        n_blk = SEQ // dil // ATT_BLK
        for r in range(dil):
            for nb in range(n_blk):
                q_rows = pl.ds(r + nb * ATT_BLK * dil, ATT_BLK, stride=dil)
                if nb == 0:
                    kv_rows, mask = q_rows, cur_mask
                else:
                    kv_rows = pl.ds(r + (nb - 1) * ATT_BLK * dil, 2 * ATT_BLK, stride=dil)
                    mask = both_mask
                q = q_ref[q_rows, :].astype(BF16)
                k = k_ref[kv_rows, :].astype(BF16)
                v = v_ref[kv_rows, :].astype(BF16)
                s = lax.dot_general(q, k, (((1,), (1,)), ((), ())), preferred_element_type=F32) * SCALE
                s = jnp.where(mask, s, NEG)
                mx = jnp.max(s, axis=-1, keepdims=True)
                p = jnp.exp(s - mx)
                den = jnp.sum(p, axis=-1, keepdims=True)
                o = jnp.dot(p.astype(BF16), v, preferred_element_type=F32) / den
                og_refs[g][q_rows, :] = o
                lg_refs[g][q_rows, :] = jnp.broadcast_to(mx + jnp.log(den), (ATT_BLK, HEAD_DIM))

    chunk = 256
    for c in range(SEQ // chunk):
        rows = pl.ds(c * chunk, chunk)
        out = _merge_groups([r_[rows, :] for r_ in og_refs], [r_[rows, :] for r_ in lg_refs])
        o_ref[rows, :] = out.astype(o_ref.dtype)


def _attn_seq(h, layer):
    m = h.shape[1]
    in_specs = []
    for g in range(N_GROUPS):
        for which in range(3):
            blk0 = (POOL_W + (3 * g + which) * ATT_W) // HEAD_DIM
            in_specs.append(pl.BlockSpec((None, SEQ, HEAD_DIM),
                                         lambda b, hd, blk0=blk0: (layer, b, blk0 + hd)))
    return pl.pallas_call(
        _attn_seq_kernel,
        grid=(m // SEQ, N_HEADS),
        in_specs=in_specs,
        out_specs=pl.BlockSpec((SEQ, HEAD_DIM), lambda b, hd: (b, hd)),
        out_shape=jax.ShapeDtypeStruct((m, ATT_W), BF16),
        scratch_shapes=[pltpu.VMEM((SEQ, HEAD_DIM), F32)] * (2 * N_GROUPS),
        compiler_params=_params(2),
        name="attn_seq",
    )(*([h] * 9))


def _attn_dec_kernel(h_ref, c1_ref, c2_ref, c3_ref, o_ref):
    def rnd(t):
        return t.astype(BF16).astype(F32)

    outs, lses = [], []
    for g, c_ref in enumerate((c1_ref, c2_ref, c3_ref)):
        r0 = (POOL_W + 3 * g * ATT_W) // HEAD_DIM
        q = rnd(h_ref[r0:r0 + N_HEADS, :])
        k_new = rnd(h_ref[r0 + N_HEADS:r0 + 2 * N_HEADS, :])
        v_new = rnd(h_ref[r0 + 2 * N_HEADS:r0 + 3 * N_HEADS, :])
        kc = rnd(c_ref[:, 0])
        vc = rnd(c_ref[:, 1])
        s = jnp.sum(kc * q[None], axis=-1, keepdims=True) * SCALE
        s_new = jnp.sum(k_new * q, axis=-1, keepdims=True) * SCALE
        mx = jnp.maximum(jnp.max(s, axis=0), s_new)
        p = jnp.exp(s - mx[None])
        p_new = jnp.exp(s_new - mx)
        den = jnp.sum(p, axis=0) + p_new
        acc = jnp.sum(rnd(p) * vc, axis=0) + rnd(p_new) * v_new
        outs.append(acc / den)
        lses.append(mx + jnp.log(den))
    o_ref[...] = _merge_groups(outs, lses)


def _attn_dec(h4, caches, layer):
    in_specs = [pl.BlockSpec((None, None, IN_W // HEAD_DIM, HEAD_DIM), lambda b: (layer, b, 0, 0))]
    for c in caches:
        in_specs.append(pl.BlockSpec((None, None, c.shape[2], None, 2, N_HEADS, HEAD_DIM),
                                     lambda b: (layer, b, 0, 0, 0, 0, 0)))
    return pl.pallas_call(
        _attn_dec_kernel,
        grid=(DEC_BATCH,),
        in_specs=in_specs,
        out_specs=pl.BlockSpec((None, N_HEADS, HEAD_DIM), lambda b: (b, 0, 0)),
        out_shape=jax.ShapeDtypeStruct((DEC_BATCH, N_HEADS, HEAD_DIM), F32),
        compiler_params=_params(1),
        name="attn_dec",
    )(h4, *caches)


def _gate_kernel(yp_ref, oa_ref, wp_ref, wa_ref, gp_ref, ga_ref, o_ref, wpb_ref, wab_ref):
    @pl.when(pl.program_id(1) == 0)
    def _cast():
        wpb_ref[...] = wp_ref[...].astype(BF16)
        wab_ref[...] = wa_ref[...].astype(BF16)

    bp = jnp.dot(yp_ref[...], wpb_ref[...], preferred_element_type=F32)
    ba = jnp.dot(oa_ref[...], wab_ref[...], preferred_element_type=F32)
    merged = jax.nn.sigmoid(gp_ref[...]) * bp + jax.nn.sigmoid(ga_ref[...]) * ba
    o_ref[...] = merged.astype(o_ref.dtype)


def _gate_merge(y_pool, o_att, h, w_br_pool, w_br_att, layer, *, tm, tn):
    m = h.shape[1]
    gp0 = GATE_OFF // tn
    ga0 = (GATE_OFF + D_MODEL) // tn
    return pl.pallas_call(
        _gate_kernel,
        grid=(D_MODEL // tn, m // tm),
        in_specs=[pl.BlockSpec((tm, POOL_W), lambda j, i: (i, 0)),
                  pl.BlockSpec((tm, ATT_W), lambda j, i: (i, 0)),
                  pl.BlockSpec((None, POOL_W, tn), lambda j, i: (layer, 0, j)),
                  pl.BlockSpec((None, ATT_W, tn), lambda j, i: (layer, 0, j)),
                  pl.BlockSpec((None, tm, tn), lambda j, i: (layer, i, gp0 + j)),
                  pl.BlockSpec((None, tm, tn), lambda j, i: (layer, i, ga0 + j))],
        out_specs=pl.BlockSpec((tm, tn), lambda j, i: (i, j)),
        out_shape=jax.ShapeDtypeStruct((m, D_MODEL), BF16),
        scratch_shapes=[pltpu.VMEM((POOL_W, tn), BF16), pltpu.VMEM((ATT_W, tn), BF16)],
        compiler_params=_params(2),
        name="gate_merge",
    )(y_pool, o_att, w_br_pool, w_br_att, h, h)


def _gelu_gate(a2, a1, a, b, cw_ref, cb_ref):
    ac = cw_ref[0:1] * a2 + cw_ref[1:2] * a1 + cw_ref[2:3] * a + cb_ref[...]
    gelu = 0.5 * ac * (1.0 + lax.erf(ac * (0.5 ** 0.5)))
    return gelu * b


def _ffn_up_kernel(*refs, tm, tiles_per_seq, sequential, n_chunks):
    if sequential:
        x_ref, wa_ref, wb_ref, cw_ref, cb_ref, hm_ref, tail_ref, wab_ref, wbb_ref, carry_ref = refs
    else:
        x_ref, wa_ref, wb_ref, cw_ref, cb_ref, p2_ref, p1_ref, hm_ref, tail_ref, wab_ref, wbb_ref = refs
    i = pl.program_id(1)

    @pl.when(i == 0)
    def _cast():
        wab_ref[...] = wa_ref[...].astype(BF16)
        wbb_ref[...] = wb_ref[...].astype(BF16)

    if not sequential:
        x = x_ref[...]
        a = jnp.dot(x, wab_ref[...], preferred_element_type=F32)
        b = jnp.dot(x, wbb_ref[...], preferred_element_type=F32)
        tail_ref[...] = a
        hm_ref[...] = _gelu_gate(p2_ref[...], p1_ref[...], a, b, cw_ref, cb_ref).astype(hm_ref.dtype)
        return

    @pl.when(i % tiles_per_seq == 0)
    def _reset():
        carry_ref[...] = jnp.zeros_like(carry_ref)

    tc = tm // n_chunks
    row = lax.broadcasted_iota(jnp.int32, (tc, 1), 0)
    carry = carry_ref[...]
    for c in range(n_chunks):
        rows = pl.ds(c * tc, tc)
        x = x_ref[rows, :]
        a = jnp.dot(x, wab_ref[...], preferred_element_type=F32)
        b = jnp.dot(x, wbb_ref[...], preferred_element_type=F32)
        a1 = jnp.where(row == 0, carry[7:8], pltpu.roll(a, 1, 0))
        a2 = jnp.where(row == 0, carry[6:7], jnp.where(row == 1, carry[7:8], pltpu.roll(a, 2, 0)))
        hm_ref[rows, :] = _gelu_gate(a2, a1, a, b, cw_ref, cb_ref).astype(hm_ref.dtype)
        carry = a[tc - 8:]
    carry_ref[...] = carry
    tail_ref[0] = carry


def _ffn_up(x, w_up, conv_w, conv_b, layer, *, tm, tn, n_chunks, prev=None):
    m, k = x.shape
    n_panels = D_FF // tn
    sequential = prev is None
    in_specs = [pl.BlockSpec((tm, k), lambda j, i: (i, 0)),
                pl.BlockSpec((None, k, tn), lambda j, i: (layer, 0, j)),
                pl.BlockSpec((None, k, tn), lambda j, i: (layer, 0, n_panels + j)),
                pl.BlockSpec((None, 3, tn), lambda j, i: (layer, 0, j)),
                pl.BlockSpec((None, 1, tn), lambda j, i: (layer, 0, j))]
    args = [x, w_up, w_up, conv_w, conv_b]
    scratch = [pltpu.VMEM((k, tn), BF16), pltpu.VMEM((k, tn), BF16)]
    if sequential:
        tail_shape = jax.ShapeDtypeStruct((m // tm, 8, D_FF), F32)
        tail_spec = pl.BlockSpec((1, 8, tn), lambda j, i: (i, 0, j))
        scratch.append(pltpu.VMEM((8, tn), F32))
    else:
        in_specs += [pl.BlockSpec((None, tm, tn), lambda j, i: (0, i, j)),
                     pl.BlockSpec((None, tm, tn), lambda j, i: (1, i, j))]
        args += [prev, prev]
        tail_shape = jax.ShapeDtypeStruct((m, D_FF), F32)
        tail_spec = pl.BlockSpec((tm, tn), lambda j, i: (i, j))
    return pl.pallas_call(
        functools.partial(_ffn_up_kernel, tm=tm, tiles_per_seq=max(SEQ // tm, 1),
                          sequential=sequential, n_chunks=n_chunks),
        grid=(n_panels, m // tm),
        in_specs=in_specs,
        out_specs=[pl.BlockSpec((tm, tn), lambda j, i: (i, j)), tail_spec],
        out_shape=[jax.ShapeDtypeStruct((m, D_FF), BF16), tail_shape],
        scratch_shapes=scratch,
        compiler_params=_params(2),
        name="ffn_up",
    )(*args)


def _layer(x, xb, h_prev, layer, p, *, dec=None):
    m = x.shape[0]
    t = _tiles(m)
    h = _proj(xb, p['w_in'], layer, n=IN_W, tm=t['proj'][0], tn=t['proj'][1],
              stack=(DEPTH, h_prev), name="in_proj")
    if dec is None:
        y_pool = _pool_seq(h, p['w_pool_grp'], p['pool_scale'], layer, tm=t['pool'])
        o_att = _attn_seq(h, layer)
    else:
        y_pool = _pool_dec(h, dec['pool'], p['w_pool_grp'], p['pool_scale'], layer)
        h4 = h.reshape(DEPTH, m, IN_W // HEAD_DIM, HEAD_DIM)
        o = _attn_dec(h4, dec['caches'], layer).reshape(DEC_BATCH, ATT_W)
        o_att = jnp.pad(o, ((0, m - DEC_BATCH), (0, 0))).astype(BF16)
    merged = _gate_merge(y_pool, o_att, h, p['w_br_pool'], p['w_br_att'], layer,
                         tm=t['gate'][0], tn=t['gate'][1])
    z = _proj(merged, p['w_out'], layer, n=D_MODEL, tm=t['proj'][0], tn=t['proj'][1],
              resid=x, alpha=ALPHA, name="out_proj")
    x1, x1b = _layer_norm(z, p['ln1_g'], p['ln1_b'], layer, tm=t['ln'])
    hm, tail = _ffn_up(x1b, p['w_up'], p['conv_w'], p['conv_b'], layer, tm=t['ffn_up'][0],
                       tn=t['ffn_up'][1], n_chunks=t['ffn_chunks'],
                       prev=None if dec is None else dec['conv'][layer])
    z2 = _down(hm, p['w_down'], layer, x1, tm=t['down'][0], tn=t['down'][1], tk=t['down'][2])
    x2, x2b = _layer_norm(z2, p['ln2_g'], p['ln2_b'], layer, tm=t['ln'])
    return x2, x2b, h, tail


def _run_prompt(x_prompt, p):
    batch = x_prompt.shape[0]
    x = x_prompt.reshape(batch * SEQ, D_MODEL)
    xb = x.astype(BF16)
    h, tails = None, []
    for l in range(DEPTH):
        x, xb, h, tail = _layer(x, xb, h, l, p)
        tails.append(tail)
    h4 = h.reshape(DEPTH, batch, SEQ, IN_W)
    pool = h4[:, :, SEQ - POOL_CTX:, :POOL_W]
    kvs = []
    for g, (win, _) in enumerate(ATT_GROUPS):
        base = POOL_W + 3 * g * ATT_W + ATT_W
        keep = min(win, SEQ)
        kvs.append(h4[:, :, SEQ - keep:, base:base + 2 * ATT_W]
                   .reshape(DEPTH, batch, keep, 2, N_HEADS, HEAD_DIM))
    tm = _tiles(batch * SEQ)['ffn_up'][0]
    t5 = jnp.stack(tails).reshape(DEPTH, batch, SEQ // tm, 8, D_FF)
    conv = t5[:, :, SEQ // tm - 1, 6:8, :]
    return (x.reshape(batch, SEQ, D_MODEL), pool, *kvs, conv)


def _run_sample(x_sample, state_pool, caches_in, state_conv, p):
    pad = DEC_ROWS - DEC_BATCH
    x = jnp.pad(x_sample.reshape(DEC_BATCH, D_MODEL), ((0, pad), (0, 0)))
    dec = dict(
        pool=jnp.pad(jnp.transpose(state_pool, (0, 2, 1, 3)), ((0, 0), (0, 0), (0, pad), (0, 0))),
        conv=jnp.pad(jnp.transpose(state_conv, (0, 2, 1, 3)), ((0, 0), (0, 0), (0, pad), (0, 0))),
        caches=[c.reshape(DEPTH, DEC_BATCH, c.shape[2] // dil, dil, 2, N_HEADS, HEAD_DIM)
                for c, (_, dil) in zip(caches_in, ATT_GROUPS)],
    )
    xb = x.astype(BF16)
    h, a_new = None, []
    for l in range(DEPTH):
        x, xb, h, a = _layer(x, xb, h, l, p, dec=dec)
        a_new.append(a[:DEC_BATCH])
    h = h[:, :DEC_BATCH]
    pool = jnp.concatenate([state_pool[:, :, 1:], h[:, :, None, :POOL_W]], axis=2)
    kvs = []
    for g, (win, _) in enumerate(ATT_GROUPS):
        base = POOL_W + 3 * g * ATT_W + ATT_W
        kv_new = h[:, :, base:base + 2 * ATT_W].reshape(DEPTH, DEC_BATCH, 1, 2, N_HEADS, HEAD_DIM)
        keep = min(win, PAST_LEN + 1)
        old = caches_in[g]
        kvs.append(jnp.concatenate([old[:, :, old.shape[2] + 1 - keep:], kv_new], axis=2))
    conv = jnp.concatenate([state_conv[:, :, 1:], jnp.stack(a_new)[:, :, None, :]], axis=2)
    return (x[:DEC_BATCH].reshape(DEC_BATCH, 1, D_MODEL), pool, *kvs, conv)


def kernel(x_prompt, x_sample, state_pool, cache_kv1, cache_kv2, cache_kv3, state_conv, w_in, w_pool_grp, pool_scale, w_br_pool, w_br_att, w_out, ln1_g, ln1_b, w_up, conv_w, conv_b, w_down, ln2_g, ln2_b):
    def row3(t):
        return t.reshape(DEPTH, 1, t.shape[-1])

    p = dict(w_in=w_in, w_pool_grp=w_pool_grp, pool_scale=row3(pool_scale), w_br_pool=w_br_pool,
             w_br_att=w_br_att, w_out=w_out, ln1_g=row3(ln1_g), ln1_b=row3(ln1_b), w_up=w_up,
             conv_w=conv_w, conv_b=row3(conv_b), w_down=w_down, ln2_g=row3(ln2_g), ln2_b=row3(ln2_b))
    yp, pool_p, kv1_p, kv2_p, kv3_p, conv_p = _run_prompt(x_prompt, p)
    ys, pool_s, kv1_s, kv2_s, kv3_s, conv_s = _run_sample(
        x_sample, state_pool, (cache_kv1, cache_kv2, cache_kv3), state_conv, p)
    return (yp, ys, pool_p, pool_s, kv1_p, kv1_s, kv2_p, kv2_s, kv3_p, kv3_s, conv_p, conv_s)
```

```python
import functools

import jax
import jax.numpy as jnp
from jax import lax
from jax.experimental import pallas as pl
from jax.experimental.pallas import tpu as pltpu

D_MODEL = 4096
SEQ = 2048
DEPTH = 2
DEC_BATCH = 8
PAST_LEN = 16384
HEAD_DIM = 128
N_HEADS = 8
ATT_GROUPS = ((128, 1), (512, 4), (2048, 16))
N_GROUPS = len(ATT_GROUPS)
ATT_W = N_HEADS * HEAD_DIM
POOL_WINDOWS = (2, 4, 8, 16)
POOL_W = D_MODEL // 2
POOL_GC = POOL_W // len(POOL_WINDOWS)
POOL_CTX = max(POOL_WINDOWS) - 1
D_FF = 11008
GATE_OFF = POOL_W + 3 * N_GROUPS * ATT_W
IN_W = GATE_OFF + 2 * D_MODEL
ALPHA = (2.0 * DEPTH) ** 0.25
LN_EPS = 1e-5
NEG = -1e30
ATT_BLK = 128
SCALE = HEAD_DIM ** -0.5

DEC_ROWS = 16
HALO = 16
VMEM_LIMIT = 62 * 1024 * 1024

F32 = jnp.float32
BF16 = jnp.bfloat16

assert all(PAST_LEN >= win and win % dil == 0 for win, dil in ATT_GROUPS)


def _tiles(m):
    big = m >= 1024
    return dict(
        proj=(512 if big else m, 1024),
        gate=(512 if big else m, 1024),
        ffn_up=(1024 if big else m, 256),
        ffn_chunks=2,
        down=(2048, 1024, 1024) if big else (m, 1024, 2048),
        ln=256 if big else m,
        pool=512,
    )


def _params(n_axes, flags=None):
    return pltpu.CompilerParams(dimension_semantics=("arbitrary",) * n_axes,
                                vmem_limit_bytes=VMEM_LIMIT, flags=flags)


def _window_copies(old_refs, new_refs, out_refs, sem):
    copies = []
    for g, (old, new, out) in enumerate(zip(old_refs, new_refs, out_refs)):
        n_old, n_new = old.shape[2], new.shape[2]
        kept = out.shape[2] - n_new
        for l in range(old.shape[0]):
            for b in range(old.shape[1]):
                copies.append(pltpu.make_async_copy(old.at[l, b, pl.ds(n_old - kept, kept)],
                                                    out.at[l, b, pl.ds(0, kept)], sem.at[g, l, b]))
        copies.append(pltpu.make_async_copy(new, out.at[:, :, pl.ds(kept, n_new)],
                                            sem.at[g, old.shape[0], 0]))
    return copies


def _proj_kernel(*refs, alpha, has_resid, has_alias, n_roll):
    refs = list(refs)
    x_ref, w_ref = refs[:2]
    r_ref = refs[2] if has_resid else None
    n_in = 2 + int(has_resid) + int(has_alias)
    o_ref = refs[n_in + 2 * n_roll]
    wb_ref = refs[n_in + 3 * n_roll + 1]
    j, i = pl.program_id(0), pl.program_id(1)

    if n_roll:
        copies = _window_copies(refs[n_in:n_in + n_roll], refs[n_in + n_roll:n_in + 2 * n_roll],
                                refs[n_in + 2 * n_roll + 1:n_in + 3 * n_roll + 1], refs[-1])

        @pl.when((j == 0) & (i == 0))
        def _start():
            for c in copies:
                c.start()

    @pl.when(i == 0)
    def _cast():
        wb_ref[...] = w_ref[...].astype(BF16)

    acc = jnp.dot(x_ref[...], wb_ref[...], preferred_element_type=F32)
    if has_resid:
        acc = alpha * r_ref[...] + acc
    if len(o_ref.shape) == 3:
        o_ref[0] = acc
        o_ref[1:] = jnp.zeros((o_ref.shape[0] - 1,) + acc.shape, o_ref.dtype)
    else:
        o_ref[...] = acc.astype(o_ref.dtype)

    if n_roll:
        @pl.when((j == pl.num_programs(0) - 1) & (i == pl.num_programs(1) - 1))
        def _wait():
            for c in copies:
                c.wait()


def _proj(x, w3, layer, *, n, tm, tn, resid=None, alpha=1.0, stack=None, roll=None, name):
    m, k = x.shape
    grid = (n // tn, m // tm)
    in_specs = [pl.BlockSpec((tm, k), lambda j, i: (i, 0)),
                pl.BlockSpec((None, k, tn), lambda j, i: (layer, 0, j))]
    args = [x, w3]
    if resid is not None:
        in_specs.append(pl.BlockSpec((tm, tn), lambda j, i: (i, j)))
        args.append(resid)
    aliases = {}
    if stack is None:
        out_shapes = [jax.ShapeDtypeStruct((m, n), F32)]
        out_specs = [pl.BlockSpec((tm, tn), lambda j, i: (i, j))]
    else:
        depth, prev = stack
        out_shapes = [jax.ShapeDtypeStruct((depth, m, n), F32)]
        out_specs = [pl.BlockSpec((None, tm, tn), lambda j, i: (layer, i, j))]
        if prev is None:
            assert layer == 0
            out_specs = [pl.BlockSpec((depth, tm, tn), lambda j, i: (0, i, j))]
        else:
            in_specs.append(pl.BlockSpec(memory_space=pl.ANY))
            aliases = {len(args): 0}
            args.append(prev)
    scratch = [pltpu.VMEM((k, tn), BF16)]
    n_roll = 0
    if roll is not None:
        olds, news = roll
        n_roll = len(olds)
        in_specs += [pl.BlockSpec(memory_space=pl.ANY)] * (2 * n_roll)
        args += list(olds) + list(news)
        out_shapes += [jax.ShapeDtypeStruct(o.shape, o.dtype) for o in olds]
        out_specs += [pl.BlockSpec(memory_space=pl.ANY)] * n_roll
        scratch.append(pltpu.SemaphoreType.DMA((n_roll, olds[0].shape[0] + 1, olds[0].shape[1])))
    outs = pl.pallas_call(
        functools.partial(_proj_kernel, alpha=alpha, has_resid=resid is not None,
                          has_alias=bool(aliases), n_roll=n_roll),
        grid=grid,
        in_specs=in_specs,
        out_specs=out_specs,
        out_shape=out_shapes,
        scratch_shapes=scratch,
        input_output_aliases=aliases,
        compiler_params=_params(2),
        name=name,
    )(*args)
    return outs[0] if roll is None else outs


def _down_kernel(x_ref, w_ref, r_ref, o_ref, *, alpha, rem):
    kk = pl.program_id(2)
    nk = pl.num_programs(2)

    @pl.when(kk == 0)
    def _first():
        o_ref[...] = alpha * r_ref[...] + jnp.dot(x_ref[...], w_ref[...].astype(BF16),
                                                  preferred_element_type=F32)

    @pl.when((kk > 0) & (kk < nk - 1))
    def _full():
        o_ref[...] += jnp.dot(x_ref[...], w_ref[...].astype(BF16), preferred_element_type=F32)

    @pl.when(kk == nk - 1)
    def _last():
        o_ref[...] += jnp.dot(x_ref[:, :rem], w_ref[:rem, :].astype(BF16), preferred_element_type=F32)


def _down(x, w3, layer, resid, *, tm, tn, tk):
    m, k = x.shape
    n = w3.shape[2]
    nk = pl.cdiv(k, tk)
    assert nk >= 2
    rem = k - (nk - 1) * tk
    return pl.pallas_call(
        functools.partial(_down_kernel, alpha=ALPHA, rem=rem),
        grid=(n // tn, m // tm, nk),
        in_specs=[pl.BlockSpec((tm, tk), lambda j, i, kk: (i, kk)),
                  pl.BlockSpec((None, tk, tn), lambda j, i, kk: (layer, kk, j)),
                  pl.BlockSpec((tm, tn), lambda j, i, kk: (i, j))],
        out_specs=pl.BlockSpec((tm, tn), lambda j, i, kk: (i, j)),
        out_shape=jax.ShapeDtypeStruct((m, n), F32),
        compiler_params=_params(3),
        name="down_proj",
    )(x, w3, resid)


def _ln_kernel(z_ref, g_ref, b_ref, o_ref, ob_ref):
    z = z_ref[...]
    mu = jnp.mean(z, axis=-1, keepdims=True)
    zc = z - mu
    var = jnp.mean(zc * zc, axis=-1, keepdims=True)
    y = zc * lax.rsqrt(var + LN_EPS) * g_ref[...] + b_ref[...]
    o_ref[...] = y
    ob_ref[...] = y.astype(BF16)


def _layer_norm(z, g3, b3, layer, *, tm):
    m, d = z.shape
    return pl.pallas_call(
        _ln_kernel,
        grid=(m // tm,),
        in_specs=[pl.BlockSpec((tm, d), lambda i: (i, 0)),
                  pl.BlockSpec((None, 1, d), lambda i: (layer, 0, 0)),
                  pl.BlockSpec((None, 1, d), lambda i: (layer, 0, 0))],
        out_specs=[pl.BlockSpec((tm, d), lambda i: (i, 0)),
                   pl.BlockSpec((tm, d), lambda i: (i, 0))],
        out_shape=[jax.ShapeDtypeStruct((m, d), F32), jax.ShapeDtypeStruct((m, d), BF16)],
        compiler_params=_params(1),
        name="layer_norm",
    )(z, g3, b3)


def _pool_groups(window_sum_fn, u_fn, inv_cnt_fn, wb_ref, sc_ref, y_ref):
    for g, w in enumerate(POOL_WINDOWS):
        cols = slice(g * POOL_GC, (g + 1) * POOL_GC)
        d = window_sum_fn(g, w, cols) * inv_cnt_fn(w) - u_fn(cols)
        y = jnp.dot(d.astype(BF16), wb_ref[g], preferred_element_type=F32) * sc_ref[:, cols]
        y_ref[:, cols] = y.astype(y_ref.dtype)


def _pool_seq_kernel(u_ref, halo_ref, wg_ref, sc_ref, y_ref, wb_ref, *, tm, tiles_per_seq):
    i = pl.program_id(0)

    @pl.when(i == 0)
    def _cast():
        wb_ref[...] = wg_ref[...].astype(BF16)

    t_in_seq = i % tiles_per_seq
    halo = jnp.where(t_in_seq == 0, 0.0, halo_ref[...])
    pos = t_in_seq * tm + lax.broadcasted_iota(jnp.int32, (tm, 1), 0)

    def window_sum(g, w, cols):
        s = jnp.concatenate([halo[:, cols], u_ref[:, cols]], axis=0)
        k = 1
        while k < w:
            s = s + pltpu.roll(s, k, 0)
            k *= 2
        return s[HALO:]

    def inv_cnt(w):
        return 1.0 / jnp.minimum(pos + 1, w).astype(F32)

    _pool_groups(window_sum, lambda cols: u_ref[:, cols], inv_cnt, wb_ref, sc_ref, y_ref)


def _pool_seq(h, w_pool_grp, pool_scale, layer, *, tm):
    m = h.shape[1]
    tiles_per_seq = SEQ // tm
    return pl.pallas_call(
        functools.partial(_pool_seq_kernel, tm=tm, tiles_per_seq=tiles_per_seq),
        grid=(m // tm,),
        in_specs=[pl.BlockSpec((None, tm, POOL_W), lambda i: (layer, i, 0)),
                  pl.BlockSpec((None, HALO, POOL_W),
                               lambda i: (layer, jnp.maximum(i * (tm // HALO) - 1, 0), 0)),
                  pl.BlockSpec((None, len(POOL_WINDOWS), POOL_GC, POOL_GC), lambda i: (layer, 0, 0, 0)),
                  pl.BlockSpec((None, 1, POOL_W), lambda i: (layer, 0, 0))],
        out_specs=pl.BlockSpec((tm, POOL_W), lambda i: (i, 0)),
        out_shape=jax.ShapeDtypeStruct((m, POOL_W), BF16),
        scratch_shapes=[pltpu.VMEM((len(POOL_WINDOWS), POOL_GC, POOL_GC), BF16)],
        compiler_params=_params(1),
        name="pool_seq",
    )(h, h, w_pool_grp, pool_scale)


def _pool_dec_kernel(u_ref, st_ref, wg_ref, sc_ref, y_ref, wb_ref):
    wb_ref[...] = wg_ref[...].astype(BF16)

    def window_sum(g, w, cols):
        s = u_ref[:, cols]
        for k in range(1, w):
            s = s + st_ref[POOL_CTX - k, :, cols]
        return s

    def inv_cnt(w):
        return 1.0 / float(min(PAST_LEN + 1, w))

    _pool_groups(window_sum, lambda cols: u_ref[:, cols], inv_cnt, wb_ref, sc_ref, y_ref)


def _pool_dec(h, st, w_pool_grp, pool_scale, layer):
    m = h.shape[1]
    return pl.pallas_call(
        _pool_dec_kernel,
        grid=(1,),
        in_specs=[pl.BlockSpec((None, m, POOL_W), lambda i: (layer, 0, 0)),
                  pl.BlockSpec((None, POOL_CTX, m, POOL_W), lambda i: (layer, 0, 0, 0)),
                  pl.BlockSpec((None, len(POOL_WINDOWS), POOL_GC, POOL_GC), lambda i: (layer, 0, 0, 0)),
                  pl.BlockSpec((None, 1, POOL_W), lambda i: (layer, 0, 0))],
        out_specs=pl.BlockSpec((m, POOL_W), lambda i: (0, 0)),
        out_shape=jax.ShapeDtypeStruct((m, POOL_W), BF16),
        scratch_shapes=[pltpu.VMEM((len(POOL_WINDOWS), POOL_GC, POOL_GC), BF16)],
        compiler_params=_params(1),
        name="pool_dec",
    )(h, st, w_pool_grp, pool_scale)


def _merge_groups(o_list, lse_list):
    mx = functools.reduce(jnp.maximum, lse_list)
    ws = [jnp.exp(l - mx) for l in lse_list]
    num = functools.reduce(lambda a, b: a + b, [w * o for w, o in zip(ws, o_list)])
    den = functools.reduce(lambda a, b: a + b, ws)
    return num / den


def _attn_seq_kernel(*refs):
    qkv = refs[:9]
    o_ref = refs[9]
    og_refs = refs[10:13]
    lg_refs = refs[13:16]

    qi = lax.broadcasted_iota(jnp.int32, (ATT_BLK, ATT_BLK), 0)
    kj = lax.broadcasted_iota(jnp.int32, (ATT_BLK, ATT_BLK), 1)
    cur_mask = kj <= qi
    qi2 = lax.broadcasted_iota(jnp.int32, (ATT_BLK, 2 * ATT_BLK), 0)
    kj2 = lax.broadcasted_iota(jnp.int32, (ATT_BLK, 2 * ATT_BLK), 1)
    both_mask = (kj2 >= qi2) & (kj2 - qi2 <= ATT_BLK)

    for g, (win, dil) in enumerate(ATT_GROUPS):
        q_ref, k_ref, v_ref = qkv[3 * g:3 * g + 3]
        n_blk = SEQ // dil // ATT_BLK
        for r in range(dil):
            for nb in range(n_blk):
                q_rows = pl.ds(r + nb * ATT_BLK * dil, ATT_BLK, stride=dil)
                if nb == 0:
                    kv_rows, mask = q_rows, cur_mask
                else:
                    kv_rows = pl.ds(r + (nb - 1) * ATT_BLK * dil, 2 * ATT_BLK, stride=dil)
                    mask = both_mask
                q = q_ref[q_rows, :].astype(BF16)
                k = k_ref[kv_rows, :].astype(BF16)
                v = v_ref[kv_rows, :].astype(BF16)
                s = lax.dot_general(q, k, (((1,), (1,)), ((), ())), preferred_element_type=F32) * SCALE
                s = jnp.where(mask, s, NEG)
                mx = jnp.max(s, axis=-1, keepdims=True)
                p = jnp.exp(s - mx)
                den = jnp.sum(p, axis=-1, keepdims=True)
                o = jnp.dot(p.astype(BF16), v, preferred_element_type=F32) / den
                og_refs[g][q_rows, :] = o
                lg_refs[g][q_rows, :] = jnp.broadcast_to(mx + jnp.log(den), (ATT_BLK, HEAD_DIM))

    chunk = 256
    for c in range(SEQ // chunk):
        rows = pl.ds(c * chunk, chunk)
        out = _merge_groups([r_[rows, :] for r_ in og_refs], [r_[rows, :] for r_ in lg_refs])
        o_ref[rows, :] = out.astype(o_ref.dtype)


def _attn_seq(h, layer):
    m = h.shape[1]
    in_specs = []
    for g in range(N_GROUPS):
        for which in range(3):
            blk0 = (POOL_W + (3 * g + which) * ATT_W) // HEAD_DIM
            in_specs.append(pl.BlockSpec((None, SEQ, HEAD_DIM),
                                         lambda b, hd, blk0=blk0: (layer, b, blk0 + hd)))
    return pl.pallas_call(
        _attn_seq_kernel,
        grid=(m // SEQ, N_HEADS),
        in_specs=in_specs,
        out_specs=pl.BlockSpec((SEQ, HEAD_DIM), lambda b, hd: (b, hd)),
        out_shape=jax.ShapeDtypeStruct((m, ATT_W), BF16),
        scratch_shapes=[pltpu.VMEM((SEQ, HEAD_DIM), F32)] * (2 * N_GROUPS),
        compiler_params=_params(2),
        name="attn_seq",
    )(*([h] * 9))


def _attn_dec_kernel(h_ref, c1_ref, c2_ref, c3_ref, o_ref):
    def rnd(t):
        return t.astype(BF16).astype(F32)

    outs, lses = [], []
    for g, c_ref in enumerate((c1_ref, c2_ref, c3_ref)):
        r0 = (POOL_W + 3 * g * ATT_W) // HEAD_DIM
        q = rnd(h_ref[r0:r0 + N_HEADS, :])
        k_new = rnd(h_ref[r0 + N_HEADS:r0 + 2 * N_HEADS, :])
        v_new = rnd(h_ref[r0 + 2 * N_HEADS:r0 + 3 * N_HEADS, :])
        kc = rnd(c_ref[:, 0])
        vc = rnd(c_ref[:, 1])
        s = jnp.sum(kc * q[None], axis=-1, keepdims=True) * SCALE
        s_new = jnp.sum(k_new * q, axis=-1, keepdims=True) * SCALE
        mx = jnp.maximum(jnp.max(s, axis=0), s_new)
        p = jnp.exp(s - mx[None])
        p_new = jnp.exp(s_new - mx)
        den = jnp.sum(p, axis=0) + p_new
        acc = jnp.sum(rnd(p) * vc, axis=0) + rnd(p_new) * v_new
        outs.append(acc / den)
        lses.append(mx + jnp.log(den))
    o_ref[...] = _merge_groups(outs, lses)


def _attn_dec(h4, caches, layer):
    in_specs = [pl.BlockSpec((None, None, IN_W // HEAD_DIM, HEAD_DIM), lambda b: (layer, b, 0, 0))]
    for c in caches:
        in_specs.append(pl.BlockSpec((None, None, c.shape[2], None, 2, N_HEADS, HEAD_DIM),
                                     lambda b: (layer, b, 0, 0, 0, 0, 0)))
    return pl.pallas_call(
        _attn_dec_kernel,
        grid=(DEC_BATCH,),
        in_specs=in_specs,
        out_specs=pl.BlockSpec((None, N_HEADS, HEAD_DIM), lambda b: (b, 0, 0)),
        out_shape=jax.ShapeDtypeStruct((DEC_BATCH, N_HEADS, HEAD_DIM), F32),
        compiler_params=_params(1),
        name="attn_dec",
    )(h4, *caches)


def _gate_kernel(yp_ref, oa_ref, wp_ref, wa_ref, gp_ref, ga_ref, o_ref, wpb_ref, wab_ref):
    @pl.when(pl.program_id(1) == 0)
    def _cast():
        wpb_ref[...] = wp_ref[...].astype(BF16)
        wab_ref[...] = wa_ref[...].astype(BF16)

    bp = jnp.dot(yp_ref[...], wpb_ref[...], preferred_element_type=F32)
    ba = jnp.dot(oa_ref[...], wab_ref[...], preferred_element_type=F32)
    merged = jax.nn.sigmoid(gp_ref[...]) * bp + jax.nn.sigmoid(ga_ref[...]) * ba
    o_ref[...] = merged.astype(o_ref.dtype)


def _gate_merge(y_pool, o_att, h, w_br_pool, w_br_att, layer, *, tm, tn):
    m = h.shape[1]
    gp0 = GATE_OFF // tn
    ga0 = (GATE_OFF + D_MODEL) // tn
    return pl.pallas_call(
        _gate_kernel,
        grid=(D_MODEL // tn, m // tm),
        in_specs=[pl.BlockSpec((tm, POOL_W), lambda j, i: (i, 0)),
                  pl.BlockSpec((tm, ATT_W), lambda j, i: (i, 0)),
                  pl.BlockSpec((None, POOL_W, tn), lambda j, i: (layer, 0, j)),
                  pl.BlockSpec((None, ATT_W, tn), lambda j, i: (layer, 0, j)),
                  pl.BlockSpec((None, tm, tn), lambda j, i: (layer, i, gp0 + j)),
                  pl.BlockSpec((None, tm, tn), lambda j, i: (layer, i, ga0 + j))],
        out_specs=pl.BlockSpec((tm, tn), lambda j, i: (i, j)),
        out_shape=jax.ShapeDtypeStruct((m, D_MODEL), BF16),
        scratch_shapes=[pltpu.VMEM((POOL_W, tn), BF16), pltpu.VMEM((ATT_W, tn), BF16)],
        compiler_params=_params(2),
        name="gate_merge",
    )(y_pool, o_att, w_br_pool, w_br_att, h, h)


def _gelu_gate(a2, a1, a, b, cw_ref, cb_ref):
    ac = cw_ref[0:1] * a2 + cw_ref[1:2] * a1 + cw_ref[2:3] * a + cb_ref[...]
    gelu = 0.5 * ac * (1.0 + lax.erf(ac * (0.5 ** 0.5)))
    return gelu * b


def _ffn_up_seq_kernel(x_ref, wa_ref, wb_ref, cw_ref, cb_ref, hm_ref, tail_ref,
                       wab_ref, wbb_ref, carry_ref, *, tm, tiles_per_seq, n_chunks):
    i = pl.program_id(1)

    @pl.when(i == 0)
    def _cast():
        wab_ref[...] = wa_ref[...].astype(BF16)
        wbb_ref[...] = wb_ref[...].astype(BF16)

    @pl.when(i % tiles_per_seq == 0)
    def _reset():
        carry_ref[...] = jnp.zeros_like(carry_ref)

    tc = tm // n_chunks
    row = lax.broadcasted_iota(jnp.int32, (tc, 1), 0)
    carry = carry_ref[...]
    for c in range(n_chunks):
        rows = pl.ds(c * tc, tc)
        x = x_ref[rows, :]
        a = jnp.dot(x, wab_ref[...], preferred_element_type=F32)
        b = jnp.dot(x, wbb_ref[...], preferred_element_type=F32)
        a1 = jnp.where(row == 0, carry[7:8], pltpu.roll(a, 1, 0))
        a2 = jnp.where(row == 0, carry[6:7], jnp.where(row == 1, carry[7:8], pltpu.roll(a, 2, 0)))
        hm_ref[rows, :] = _gelu_gate(a2, a1, a, b, cw_ref, cb_ref).astype(hm_ref.dtype)
        carry = a[tc - 8:]
    carry_ref[...] = carry
    tail_ref[0] = carry


def _ffn_up_seq(x, w_up, conv_w, conv_b, layer, *, tm, tn, n_chunks):
    m, k = x.shape
    n_panels = D_FF // tn
    return pl.pallas_call(
        functools.partial(_ffn_up_seq_kernel, tm=tm, tiles_per_seq=SEQ // tm, n_chunks=n_chunks),
        grid=(n_panels, m // tm),
        in_specs=[pl.BlockSpec((tm, k), lambda j, i: (i, 0)),
                  pl.BlockSpec((None, k, tn), lambda j, i: (layer, 0, j)),
                  pl.BlockSpec((None, k, tn), lambda j, i: (layer, 0, n_panels + j)),
                  pl.BlockSpec((None, 3, tn), lambda j, i: (layer, 0, j)),
                  pl.BlockSpec((None, 1, tn), lambda j, i: (layer, 0, j))],
        out_specs=[pl.BlockSpec((tm, tn), lambda j, i: (i, j)),
                   pl.BlockSpec((1, 8, tn), lambda j, i: (i, 0, j))],
        out_shape=[jax.ShapeDtypeStruct((m, D_FF), BF16),
                   jax.ShapeDtypeStruct((m // tm, 8, D_FF), F32)],
        scratch_shapes=[pltpu.VMEM((k, tn), BF16), pltpu.VMEM((k, tn), BF16),
                        pltpu.VMEM((8, tn), F32)],
        compiler_params=_params(2),
        name="ffn_up",
    )(x, w_up, w_up, conv_w, conv_b)


def _ffn_up_dec_kernel(x_ref, wa_ref, wb_ref, cw_ref, cb_ref, p2_ref, p1_ref, hm_ref, a_ref):
    x = x_ref[...]
    a = jnp.dot(x, wa_ref[...].astype(BF16), preferred_element_type=F32)
    b = jnp.dot(x, wb_ref[...].astype(BF16), preferred_element_type=F32)
    a_ref[...] = a
    hm_ref[...] = _gelu_gate(p2_ref[...], p1_ref[...], a, b, cw_ref, cb_ref).astype(hm_ref.dtype)


def _ffn_up_dec(x, w_up, conv_w, conv_b, layer, prev, *, tn):
    m, k = x.shape
    n_panels = D_FF // tn
    return pl.pallas_call(
        _ffn_up_dec_kernel,
        grid=(n_panels,),
        in_specs=[pl.BlockSpec((m, k), lambda j: (0, 0)),
                  pl.BlockSpec((None, k, tn), lambda j: (layer, 0, j)),
                  pl.BlockSpec((None, k, tn), lambda j: (layer, 0, n_panels + j)),
                  pl.BlockSpec((None, 3, tn), lambda j: (layer, 0, j)),
                  pl.BlockSpec((None, 1, tn), lambda j: (layer, 0, j)),
                  pl.BlockSpec((None, m, tn), lambda j: (0, 0, j)),
                  pl.BlockSpec((None, m, tn), lambda j: (1, 0, j))],
        out_specs=[pl.BlockSpec((m, tn), lambda j: (0, j)), pl.BlockSpec((m, tn), lambda j: (0, j))],
        out_shape=[jax.ShapeDtypeStruct((m, D_FF), BF16), jax.ShapeDtypeStruct((m, D_FF), F32)],
        compiler_params=_params(1),
        name="ffn_up_dec",
    )(x, w_up, w_up, conv_w, conv_b, prev, prev)


def _layer(x, xb, h_prev, layer, p, *, dec=None, roll=None):
    m = x.shape[0]
    t = _tiles(m)
    h = _proj(xb, p['w_in'], layer, n=IN_W, tm=t['proj'][0], tn=t['proj'][1],
              stack=(DEPTH, h_prev), roll=roll, name="in_proj")
    rolled = None
    if roll is not None:
        h, rolled = h[0], h[1:]
    if dec is None:
        y_pool = _pool_seq(h, p['w_pool_grp'], p['pool_scale'], layer, tm=t['pool'])
        o_att = _attn_seq(h, layer)
    else:
        y_pool = _pool_dec(h, dec['pool'], p['w_pool_grp'], p['pool_scale'], layer)
        h4 = h.reshape(DEPTH, m, IN_W // HEAD_DIM, HEAD_DIM)
        o = _attn_dec(h4, dec['caches'], layer).reshape(DEC_BATCH, ATT_W)
        o_att = jnp.pad(o, ((0, m - DEC_BATCH), (0, 0))).astype(BF16)
    merged = _gate_merge(y_pool, o_att, h, p['w_br_pool'], p['w_br_att'], layer,
                         tm=t['gate'][0], tn=t['gate'][1])
    z = _proj(merged, p['w_out'], layer, n=D_MODEL, tm=t['proj'][0], tn=t['proj'][1],
              resid=x, alpha=ALPHA, name="out_proj")
    x1, x1b = _layer_norm(z, p['ln1_g'], p['ln1_b'], layer, tm=t['ln'])
    if dec is None:
        hm, tail = _ffn_up_seq(x1b, p['w_up'], p['conv_w'], p['conv_b'], layer,
                               tm=t['ffn_up'][0], tn=t['ffn_up'][1], n_chunks=t['ffn_chunks'])
    else:
        hm, tail = _ffn_up_dec(x1b, p['w_up'], p['conv_w'], p['conv_b'], layer, dec['conv'][layer],
                               tn=t['ffn_up'][1])
    z2 = _down(hm, p['w_down'], layer, x1, tm=t['down'][0], tn=t['down'][1], tk=t['down'][2])
    x2, x2b = _layer_norm(z2, p['ln2_g'], p['ln2_b'], layer, tm=t['ln'])
    return x2, x2b, h, tail, rolled


def _run_prompt(x_prompt, p, roll=None):
    batch = x_prompt.shape[0]
    x = x_prompt.reshape(batch * SEQ, D_MODEL)
    xb = x.astype(BF16)
    h, tails, rolled = None, [], None
    for l in range(DEPTH):
        x, xb, h, tail, r = _layer(x, xb, h, l, p, roll=roll if l == 0 else None)
        rolled = r if l == 0 else rolled
        tails.append(tail)
    h4 = h.reshape(DEPTH, batch, SEQ, IN_W)
    pool = h4[:, :, SEQ - POOL_CTX:, :POOL_W]
    kvs = []
    for g, (win, _) in enumerate(ATT_GROUPS):
        base = POOL_W + 3 * g * ATT_W + ATT_W
        keep = min(win, SEQ)
        kvs.append(h4[:, :, SEQ - keep:, base:base + 2 * ATT_W]
                   .reshape(DEPTH, batch, keep, 2, N_HEADS, HEAD_DIM))
    tm = _tiles(batch * SEQ)['ffn_up'][0]
    t5 = jnp.stack(tails).reshape(DEPTH, batch, SEQ // tm, 8, D_FF)
    conv = t5[:, :, SEQ // tm - 1, 6:8, :]
    return (x.reshape(batch, SEQ, D_MODEL), pool, *kvs, conv), rolled


def _run_sample(x_sample, state_pool, caches_in, state_conv, p):
    pad = DEC_ROWS - DEC_BATCH
    x = jnp.pad(x_sample.reshape(DEC_BATCH, D_MODEL), ((0, pad), (0, 0)))
    dec = dict(
        pool=jnp.pad(jnp.transpose(state_pool, (0, 2, 1, 3)), ((0, 0), (0, 0), (0, pad), (0, 0))),
        conv=jnp.pad(jnp.transpose(state_conv, (0, 2, 1, 3)), ((0, 0), (0, 0), (0, pad), (0, 0))),
        caches=[c.reshape(DEPTH, DEC_BATCH, c.shape[2] // dil, dil, 2, N_HEADS, HEAD_DIM)
                for c, (_, dil) in zip(caches_in, ATT_GROUPS)],
    )
    xb = x.astype(BF16)
    h, a_new = None, []
    for l in range(DEPTH):
        x, xb, h, a, _ = _layer(x, xb, h, l, p, dec=dec)
        a_new.append(a[:DEC_BATCH])
    h = h[:, :DEC_BATCH]
    pool = jnp.concatenate([state_pool[:, :, 1:], h[:, :, None, :POOL_W]], axis=2)
    kv_new = []
    for g in range(N_GROUPS):
        base = POOL_W + 3 * g * ATT_W + ATT_W
        kv_new.append(h[:, :, base:base + 2 * ATT_W].reshape(DEPTH, DEC_BATCH, 1, 2, N_HEADS, HEAD_DIM))
    conv = jnp.concatenate([state_conv[:, :, 1:], jnp.stack(a_new)[:, :, None, :]], axis=2)
    return (x[:DEC_BATCH].reshape(DEC_BATCH, 1, D_MODEL), pool, conv), kv_new


def kernel(x_prompt, x_sample, state_pool, cache_kv1, cache_kv2, cache_kv3, state_conv, w_in, w_pool_grp, pool_scale, w_br_pool, w_br_att, w_out, ln1_g, ln1_b, w_up, conv_w, conv_b, w_down, ln2_g, ln2_b):
    def row3(t):
        return t.reshape(DEPTH, 1, t.shape[-1])

    p = dict(w_in=w_in, w_pool_grp=w_pool_grp, pool_scale=row3(pool_scale), w_br_pool=w_br_pool,
             w_br_att=w_br_att, w_out=w_out, ln1_g=row3(ln1_g), ln1_b=row3(ln1_b), w_up=w_up,
             conv_w=conv_w, conv_b=row3(conv_b), w_down=w_down, ln2_g=row3(ln2_g), ln2_b=row3(ln2_b))
    caches = (cache_kv1, cache_kv2, cache_kv3)
    (ys, pool_s, conv_s), kv_new = _run_sample(x_sample, state_pool, caches, state_conv, p)
    (yp, pool_p, kv1_p, kv2_p, kv3_p, conv_p), (kv1_s, kv2_s, kv3_s) = _run_prompt(
        x_prompt, p, roll=(caches, kv_new))
    return (yp, ys, pool_p, pool_s, kv1_p, kv1_s, kv2_p, kv2_s, kv3_p, kv3_s, conv_p, conv_s)
```

```python
import functools

import jax
import jax.numpy as jnp
from jax import lax
from jax.experimental import pallas as pl
from jax.experimental.pallas import tpu as pltpu

D_MODEL = 4096
SEQ = 2048
DEPTH = 2
DEC_BATCH = 8
PAST_LEN = 16384
HEAD_DIM = 128
N_HEADS = 8
ATT_GROUPS = ((128, 1), (512, 4), (2048, 16))
N_GROUPS = len(ATT_GROUPS)
ATT_W = N_HEADS * HEAD_DIM
POOL_WINDOWS = (2, 4, 8, 16)
POOL_W = D_MODEL // 2
POOL_GC = POOL_W // len(POOL_WINDOWS)
POOL_CTX = max(POOL_WINDOWS) - 1
D_FF = 11008
GATE_OFF = POOL_W + 3 * N_GROUPS * ATT_W
IN_W = GATE_OFF + 2 * D_MODEL
ALPHA = (2.0 * DEPTH) ** 0.25
LN_EPS = 1e-5
NEG = -1e30
ATT_BLK = 128
SCALE = HEAD_DIM ** -0.5

DEC_ROWS = 16
HALO = 16
VMEM_LIMIT = 62 * 1024 * 1024

F32 = jnp.float32
BF16 = jnp.bfloat16

assert all(PAST_LEN >= win and win % dil == 0 for win, dil in ATT_GROUPS)


def _tiles(m):
    big = m >= 1024
    return dict(
        proj=(512 if big else m, 1024),
        gate=(512 if big else m, 1024),
        ffn_up=(2048 if big else m, 256),
        ffn_chunks=4,
        down=(2048, 1024, 1024) if big else (m, 1024, 2048),
        ln=256 if big else m,
        pool=512,
    )


def _params(n_axes, flags=None):
    return pltpu.CompilerParams(dimension_semantics=("arbitrary",) * n_axes,
                                vmem_limit_bytes=VMEM_LIMIT, flags=flags)


def _proj_kernel(*refs, alpha, has_resid):
    refs = list(refs)
    x_ref, w_ref = refs[:2]
    r_ref = refs[2] if has_resid else None
    o_ref, wb_ref = refs[-2:]

    @pl.when(pl.program_id(1) == 0)
    def _cast():
        wb_ref[...] = w_ref[...].astype(BF16)

    acc = jnp.dot(x_ref[...], wb_ref[...], preferred_element_type=F32)
    if has_resid:
        acc = alpha * r_ref[...] + acc
    if len(o_ref.shape) == 3:
        o_ref[0] = acc
        o_ref[1:] = jnp.zeros((o_ref.shape[0] - 1,) + acc.shape, o_ref.dtype)
    else:
        o_ref[...] = acc.astype(o_ref.dtype)


def _proj(x, w3, layer, *, n, tm, tn, resid=None, alpha=1.0, stack=None, name):
    m, k = x.shape
    grid = (n // tn, m // tm)
    in_specs = [pl.BlockSpec((tm, k), lambda j, i: (i, 0)),
                pl.BlockSpec((None, k, tn), lambda j, i: (layer, 0, j))]
    args = [x, w3]
    if resid is not None:
        in_specs.append(pl.BlockSpec((tm, tn), lambda j, i: (i, j)))
        args.append(resid)
    aliases = {}
    if stack is None:
        out_shapes = [jax.ShapeDtypeStruct((m, n), F32)]
        out_specs = [pl.BlockSpec((tm, tn), lambda j, i: (i, j))]
    else:
        depth, prev = stack
        out_shapes = [jax.ShapeDtypeStruct((depth, m, n), F32)]
        out_specs = [pl.BlockSpec((None, tm, tn), lambda j, i: (layer, i, j))]
        if prev is None:
            assert layer == 0
            out_specs = [pl.BlockSpec((depth, tm, tn), lambda j, i: (0, i, j))]
        else:
            in_specs.append(pl.BlockSpec(memory_space=pl.ANY))
            aliases = {len(args): 0}
            args.append(prev)
    return pl.pallas_call(
        functools.partial(_proj_kernel, alpha=alpha, has_resid=resid is not None),
        grid=grid,
        in_specs=in_specs,
        out_specs=out_specs[0],
        out_shape=out_shapes[0],
        scratch_shapes=[pltpu.VMEM((k, tn), BF16)],
        input_output_aliases=aliases,
        compiler_params=_params(2),
        name=name,
    )(*args)


def _down_kernel(x_ref, w_ref, r_ref, o_ref, *, alpha, rem):
    kk = pl.program_id(2)
    nk = pl.num_programs(2)

    @pl.when(kk == 0)
    def _first():
        o_ref[...] = alpha * r_ref[...] + jnp.dot(x_ref[...], w_ref[...].astype(BF16),
                                                  preferred_element_type=F32)

    @pl.when((kk > 0) & (kk < nk - 1))
    def _full():
        o_ref[...] += jnp.dot(x_ref[...], w_ref[...].astype(BF16), preferred_element_type=F32)

    @pl.when(kk == nk - 1)
    def _last():
        o_ref[...] += jnp.dot(x_ref[:, :rem], w_ref[:rem, :].astype(BF16), preferred_element_type=F32)


def _down(x, w3, layer, resid, *, tm, tn, tk):
    m, k = x.shape
    n = w3.shape[2]
    nk = pl.cdiv(k, tk)
    assert nk >= 2
    rem = k - (nk - 1) * tk
    return pl.pallas_call(
        functools.partial(_down_kernel, alpha=ALPHA, rem=rem),
        grid=(n // tn, m // tm, nk),
        in_specs=[pl.BlockSpec((tm, tk), lambda j, i, kk: (i, kk)),
                  pl.BlockSpec((None, tk, tn), lambda j, i, kk: (layer, kk, j)),
                  pl.BlockSpec((tm, tn), lambda j, i, kk: (i, j))],
        out_specs=pl.BlockSpec((tm, tn), lambda j, i, kk: (i, j)),
        out_shape=jax.ShapeDtypeStruct((m, n), F32),
        compiler_params=_params(3),
        name="down_proj",
    )(x, w3, resid)


def _ln_kernel(z_ref, g_ref, b_ref, o_ref, ob_ref):
    z = z_ref[...]
    mu = jnp.mean(z, axis=-1, keepdims=True)
    zc = z - mu
    var = jnp.mean(zc * zc, axis=-1, keepdims=True)
    y = zc * lax.rsqrt(var + LN_EPS) * g_ref[...] + b_ref[...]
    o_ref[...] = y
    ob_ref[...] = y.astype(BF16)


def _layer_norm(z, g3, b3, layer, *, tm):
    m, d = z.shape
    return pl.pallas_call(
        _ln_kernel,
        grid=(m // tm,),
        in_specs=[pl.BlockSpec((tm, d), lambda i: (i, 0)),
                  pl.BlockSpec((None, 1, d), lambda i: (layer, 0, 0)),
                  pl.BlockSpec((None, 1, d), lambda i: (layer, 0, 0))],
        out_specs=[pl.BlockSpec((tm, d), lambda i: (i, 0)),
                   pl.BlockSpec((tm, d), lambda i: (i, 0))],
        out_shape=[jax.ShapeDtypeStruct((m, d), F32), jax.ShapeDtypeStruct((m, d), BF16)],
        compiler_params=_params(1),
        name="layer_norm",
    )(z, g3, b3)


def _pool_groups(window_sum_fn, u_fn, inv_cnt_fn, wb_ref, sc_ref, y_ref):
    for g, w in enumerate(POOL_WINDOWS):
        cols = slice(g * POOL_GC, (g + 1) * POOL_GC)
        d = window_sum_fn(g, w, cols) * inv_cnt_fn(w) - u_fn(cols)
        y = jnp.dot(d.astype(BF16), wb_ref[g], preferred_element_type=F32) * sc_ref[:, cols]
        y_ref[:, cols] = y.astype(y_ref.dtype)


def _pool_seq_kernel(u_ref, halo_ref, wg_ref, sc_ref, y_ref, wb_ref, *, tm, tiles_per_seq):
    i = pl.program_id(0)

    @pl.when(i == 0)
    def _cast():
        wb_ref[...] = wg_ref[...].astype(BF16)

    t_in_seq = i % tiles_per_seq
    halo = jnp.where(t_in_seq == 0, 0.0, halo_ref[...])
    pos = t_in_seq * tm + lax.broadcasted_iota(jnp.int32, (tm, 1), 0)

    def window_sum(g, w, cols):
        s = jnp.concatenate([halo[:, cols], u_ref[:, cols]], axis=0)
        k = 1
        while k < w:
            s = s + pltpu.roll(s, k, 0)
            k *= 2
        return s[HALO:]

    def inv_cnt(w):
        return 1.0 / jnp.minimum(pos + 1, w).astype(F32)

    _pool_groups(window_sum, lambda cols: u_ref[:, cols], inv_cnt, wb_ref, sc_ref, y_ref)


def _pool_seq(h, w_pool_grp, pool_scale, layer, *, tm):
    m = h.shape[1]
    tiles_per_seq = SEQ // tm
    return pl.pallas_call(
        functools.partial(_pool_seq_kernel, tm=tm, tiles_per_seq=tiles_per_seq),
        grid=(m // tm,),
        in_specs=[pl.BlockSpec((None, tm, POOL_W), lambda i: (layer, i, 0)),
                  pl.BlockSpec((None, HALO, POOL_W),
                               lambda i: (layer, jnp.maximum(i * (tm // HALO) - 1, 0), 0)),
                  pl.BlockSpec((None, len(POOL_WINDOWS), POOL_GC, POOL_GC), lambda i: (layer, 0, 0, 0)),
                  pl.BlockSpec((None, 1, POOL_W), lambda i: (layer, 0, 0))],
        out_specs=pl.BlockSpec((tm, POOL_W), lambda i: (i, 0)),
        out_shape=jax.ShapeDtypeStruct((m, POOL_W), BF16),
        scratch_shapes=[pltpu.VMEM((len(POOL_WINDOWS), POOL_GC, POOL_GC), BF16)],
        compiler_params=_params(1),
        name="pool_seq",
    )(h, h, w_pool_grp, pool_scale)


def _pool_dec_kernel(u_ref, st_ref, wg_ref, sc_ref, y_ref, wb_ref):
    wb_ref[...] = wg_ref[...].astype(BF16)

    def window_sum(g, w, cols):
        s = u_ref[:, cols]
        for k in range(1, w):
            s = s + st_ref[POOL_CTX - k, :, cols]
        return s

    def inv_cnt(w):
        return 1.0 / float(min(PAST_LEN + 1, w))

    _pool_groups(window_sum, lambda cols: u_ref[:, cols], inv_cnt, wb_ref, sc_ref, y_ref)


def _pool_dec(h, st, w_pool_grp, pool_scale, layer):
    m = h.shape[1]
    return pl.pallas_call(
        _pool_dec_kernel,
        grid=(1,),
        in_specs=[pl.BlockSpec((None, m, POOL_W), lambda i: (layer, 0, 0)),
                  pl.BlockSpec((None, POOL_CTX, m, POOL_W), lambda i: (layer, 0, 0, 0)),
                  pl.BlockSpec((None, len(POOL_WINDOWS), POOL_GC, POOL_GC), lambda i: (layer, 0, 0, 0)),
                  pl.BlockSpec((None, 1, POOL_W), lambda i: (layer, 0, 0))],
        out_specs=pl.BlockSpec((m, POOL_W), lambda i: (0, 0)),
        out_shape=jax.ShapeDtypeStruct((m, POOL_W), BF16),
        scratch_shapes=[pltpu.VMEM((len(POOL_WINDOWS), POOL_GC, POOL_GC), BF16)],
        compiler_params=_params(1),
        name="pool_dec",
    )(h, st, w_pool_grp, pool_scale)


def _merge_groups(o_list, lse_list):
    mx = functools.reduce(jnp.maximum, lse_list)
    ws = [jnp.exp(l - mx) for l in lse_list]
    num = functools.reduce(lambda a, b: a + b, [w * o for w, o in zip(ws, o_list)])
    den = functools.reduce(lambda a, b: a + b, ws)
    return num / den


def _attn_seq_kernel(*refs):
    qkv = refs[:9]
    o_ref = refs[9]
    og_refs = refs[10:13]
    lg_refs = refs[13:16]

    qi = lax.broadcasted_iota(jnp.int32, (ATT_BLK, ATT_BLK), 0)
    kj = lax.broadcasted_iota(jnp.int32, (ATT_BLK, ATT_BLK), 1)
    cur_mask = kj <= qi
    qi2 = lax.broadcasted_iota(jnp.int32, (ATT_BLK, 2 * ATT_BLK), 0)
    kj2 = lax.broadcasted_iota(jnp.int32, (ATT_BLK, 2 * ATT_BLK), 1)
    both_mask = (kj2 >= qi2) & (kj2 - qi2 <= ATT_BLK)

    for g, (win, dil) in enumerate(ATT_GROUPS):
        q_ref, k_ref, v_ref = qkv[3 * g:3 * g + 3]
        n_blk = SEQ // dil // ATT_BLK
        for r in range(dil):
            for nb in range(n_blk):
                q_rows = pl.ds(r + nb * ATT_BLK * dil, ATT_BLK, stride=dil)
                if nb == 0:
                    kv_rows, mask = q_rows, cur_mask
                else:
                    kv_rows = pl.ds(r + (nb - 1) * ATT_BLK * dil, 2 * ATT_BLK, stride=dil)
                    mask = both_mask
                q = q_ref[q_rows, :].astype(BF16)
                k = k_ref[kv_rows, :].astype(BF16)
                v = v_ref[kv_rows, :].astype(BF16)
                s = lax.dot_general(q, k, (((1,), (1,)), ((), ())), preferred_element_type=F32) * SCALE
                s = jnp.where(mask, s, NEG)
                mx = jnp.max(s, axis=-1, keepdims=True)
                p = jnp.exp(s - mx)
                den = jnp.sum(p, axis=-1, keepdims=True)
                o = jnp.dot(p.astype(BF16), v, preferred_element_type=F32) / den
                og_refs[g][q_rows, :] = o
                lg_refs[g][q_rows, :] = jnp.broadcast_to(mx + jnp.log(den), (ATT_BLK, HEAD_DIM))

    chunk = 256
    for c in range(SEQ // chunk):
        rows = pl.ds(c * chunk, chunk)
        out = _merge_groups([r_[rows, :] for r_ in og_refs], [r_[rows, :] for r_ in lg_refs])
        o_ref[rows, :] = out.astype(o_ref.dtype)


def _attn_seq(h, layer):
    m = h.shape[1]
    in_specs = []
    for g in range(N_GROUPS):
        for which in range(3):
            blk0 = (POOL_W + (3 * g + which) * ATT_W) // HEAD_DIM
            in_specs.append(pl.BlockSpec((None, SEQ, HEAD_DIM),
                                         lambda b, hd, blk0=blk0: (layer, b, blk0 + hd)))
    return pl.pallas_call(
        _attn_seq_kernel,
        grid=(m // SEQ, N_HEADS),
        in_specs=in_specs,
        out_specs=pl.BlockSpec((SEQ, HEAD_DIM), lambda b, hd: (b, hd)),
        out_shape=jax.ShapeDtypeStruct((m, ATT_W), BF16),
        scratch_shapes=[pltpu.VMEM((SEQ, HEAD_DIM), F32)] * (2 * N_GROUPS),
        compiler_params=_params(2),
        name="attn_seq",
    )(*([h] * 9))


def _attn_dec_kernel(h_ref, c1_ref, c2_ref, c3_ref, o_ref):
    def rnd(t):
        return t.astype(BF16).astype(F32)

    outs, lses = [], []
    for g, c_ref in enumerate((c1_ref, c2_ref, c3_ref)):
        r0 = (POOL_W + 3 * g * ATT_W) // HEAD_DIM
        q = rnd(h_ref[r0:r0 + N_HEADS, :])
        k_new = rnd(h_ref[r0 + N_HEADS:r0 + 2 * N_HEADS, :])
        v_new = rnd(h_ref[r0 + 2 * N_HEADS:r0 + 3 * N_HEADS, :])
        kc = rnd(c_ref[:, 0])
        vc = rnd(c_ref[:, 1])
        s = jnp.sum(kc * q[None], axis=-1, keepdims=True) * SCALE
        s_new = jnp.sum(k_new * q, axis=-1, keepdims=True) * SCALE
        mx = jnp.maximum(jnp.max(s, axis=0), s_new)
        p = jnp.exp(s - mx[None])
        p_new = jnp.exp(s_new - mx)
        den = jnp.sum(p, axis=0) + p_new
        acc = jnp.sum(rnd(p) * vc, axis=0) + rnd(p_new) * v_new
        outs.append(acc / den)
        lses.append(mx + jnp.log(den))
    o_ref[...] = _merge_groups(outs, lses)


def _attn_dec(h4, caches, layer):
    in_specs = [pl.BlockSpec((None, None, IN_W // HEAD_DIM, HEAD_DIM), lambda b: (layer, b, 0, 0))]
    for c in caches:
        in_specs.append(pl.BlockSpec((None, None, c.shape[2], None, 2, N_HEADS, HEAD_DIM),
                                     lambda b: (layer, b, 0, 0, 0, 0, 0)))
    return pl.pallas_call(
        _attn_dec_kernel,
        grid=(DEC_BATCH,),
        in_specs=in_specs,
        out_specs=pl.BlockSpec((None, N_HEADS, HEAD_DIM), lambda b: (b, 0, 0)),
        out_shape=jax.ShapeDtypeStruct((DEC_BATCH, N_HEADS, HEAD_DIM), F32),
        compiler_params=_params(1),
        name="attn_dec",
    )(h4, *caches)


def _gate_kernel(yp_ref, oa_ref, wp_ref, wa_ref, gp_ref, ga_ref, o_ref, wpb_ref, wab_ref):
    @pl.when(pl.program_id(1) == 0)
    def _cast():
        wpb_ref[...] = wp_ref[...].astype(BF16)
        wab_ref[...] = wa_ref[...].astype(BF16)

    bp = jnp.dot(yp_ref[...], wpb_ref[...], preferred_element_type=F32)
    ba = jnp.dot(oa_ref[...], wab_ref[...], preferred_element_type=F32)
    merged = jax.nn.sigmoid(gp_ref[...]) * bp + jax.nn.sigmoid(ga_ref[...]) * ba
    o_ref[...] = merged.astype(o_ref.dtype)


def _gate_merge(y_pool, o_att, h, w_br_pool, w_br_att, layer, *, tm, tn):
    m = h.shape[1]
    gp0 = GATE_OFF // tn
    ga0 = (GATE_OFF + D_MODEL) // tn
    return pl.pallas_call(
        _gate_kernel,
        grid=(D_MODEL // tn, m // tm),
        in_specs=[pl.BlockSpec((tm, POOL_W), lambda j, i: (i, 0)),
                  pl.BlockSpec((tm, ATT_W), lambda j, i: (i, 0)),
                  pl.BlockSpec((None, POOL_W, tn), lambda j, i: (layer, 0, j)),
                  pl.BlockSpec((None, ATT_W, tn), lambda j, i: (layer, 0, j)),
                  pl.BlockSpec((None, tm, tn), lambda j, i: (layer, i, gp0 + j)),
                  pl.BlockSpec((None, tm, tn), lambda j, i: (layer, i, ga0 + j))],
        out_specs=pl.BlockSpec((tm, tn), lambda j, i: (i, j)),
        out_shape=jax.ShapeDtypeStruct((m, D_MODEL), BF16),
        scratch_shapes=[pltpu.VMEM((POOL_W, tn), BF16), pltpu.VMEM((ATT_W, tn), BF16)],
        compiler_params=_params(2),
        name="gate_merge",
    )(y_pool, o_att, w_br_pool, w_br_att, h, h)


def _gelu_gate(a2, a1, a, b, cw_ref, cb_ref):
    ac = cw_ref[0:1] * a2 + cw_ref[1:2] * a1 + cw_ref[2:3] * a + cb_ref[...]
    gelu = 0.5 * ac * (1.0 + lax.erf(ac * (0.5 ** 0.5)))
    return gelu * b


def _ffn_up_seq_kernel(x_ref, wa_ref, wb_ref, cw_ref, cb_ref, hm_ref, tail_ref,
                       wab_ref, wbb_ref, carry_ref, *, tm, tiles_per_seq, n_chunks):
    i = pl.program_id(1)

    @pl.when(i == 0)
    def _cast():
        wab_ref[...] = wa_ref[...].astype(BF16)
        wbb_ref[...] = wb_ref[...].astype(BF16)

    @pl.when(i % tiles_per_seq == 0)
    def _reset():
        carry_ref[...] = jnp.zeros_like(carry_ref)

    tc = tm // n_chunks
    row = lax.broadcasted_iota(jnp.int32, (tc, 1), 0)
    carry = carry_ref[...]
    for c in range(n_chunks):
        rows = pl.ds(c * tc, tc)
        x = x_ref[rows, :]
        a = jnp.dot(x, wab_ref[...], preferred_element_type=F32)
        b = jnp.dot(x, wbb_ref[...], preferred_element_type=F32)
        a1 = jnp.where(row == 0, carry[7:8], pltpu.roll(a, 1, 0))
        a2 = jnp.where(row == 0, carry[6:7], jnp.where(row == 1, carry[7:8], pltpu.roll(a, 2, 0)))
        hm_ref[rows, :] = _gelu_gate(a2, a1, a, b, cw_ref, cb_ref).astype(hm_ref.dtype)
        carry = a[tc - 8:]
    carry_ref[...] = carry
    tail_ref[0] = carry


def _ffn_up_seq(x, w_up, conv_w, conv_b, layer, *, tm, tn, n_chunks):
    m, k = x.shape
    n_panels = D_FF // tn
    return pl.pallas_call(
        functools.partial(_ffn_up_seq_kernel, tm=tm, tiles_per_seq=SEQ // tm, n_chunks=n_chunks),
        grid=(n_panels, m // tm),
        in_specs=[pl.BlockSpec((tm, k), lambda j, i: (i, 0)),
                  pl.BlockSpec((None, k, tn), lambda j, i: (layer, 0, j)),
                  pl.BlockSpec((None, k, tn), lambda j, i: (layer, 0, n_panels + j)),
                  pl.BlockSpec((None, 3, tn), lambda j, i: (layer, 0, j)),
                  pl.BlockSpec((None, 1, tn), lambda j, i: (layer, 0, j))],
        out_specs=[pl.BlockSpec((tm, tn), lambda j, i: (i, j)),
                   pl.BlockSpec((1, 8, tn), lambda j, i: (i, 0, j))],
        out_shape=[jax.ShapeDtypeStruct((m, D_FF), BF16),
                   jax.ShapeDtypeStruct((m // tm, 8, D_FF), F32)],
        scratch_shapes=[pltpu.VMEM((k, tn), BF16), pltpu.VMEM((k, tn), BF16),
                        pltpu.VMEM((8, tn), F32)],
        compiler_params=_params(2),
        name="ffn_up",
    )(x, w_up, w_up, conv_w, conv_b)


def _ffn_up_dec_kernel(x_ref, wa_ref, wb_ref, cw_ref, cb_ref, p2_ref, p1_ref, hm_ref, a_ref):
    x = x_ref[...]
    a = jnp.dot(x, wa_ref[...].astype(BF16), preferred_element_type=F32)
    b = jnp.dot(x, wb_ref[...].astype(BF16), preferred_element_type=F32)
    a_ref[...] = a
    hm_ref[...] = _gelu_gate(p2_ref[...], p1_ref[...], a, b, cw_ref, cb_ref).astype(hm_ref.dtype)


def _ffn_up_dec(x, w_up, conv_w, conv_b, layer, prev, *, tn):
    m, k = x.shape
    n_panels = D_FF // tn
    return pl.pallas_call(
        _ffn_up_dec_kernel,
        grid=(n_panels,),
        in_specs=[pl.BlockSpec((m, k), lambda j: (0, 0)),
                  pl.BlockSpec((None, k, tn), lambda j: (layer, 0, j)),
                  pl.BlockSpec((None, k, tn), lambda j: (layer, 0, n_panels + j)),
                  pl.BlockSpec((None, 3, tn), lambda j: (layer, 0, j)),
                  pl.BlockSpec((None, 1, tn), lambda j: (layer, 0, j)),
                  pl.BlockSpec((None, m, tn), lambda j: (0, 0, j)),
                  pl.BlockSpec((None, m, tn), lambda j: (1, 0, j))],
        out_specs=[pl.BlockSpec((m, tn), lambda j: (0, j)), pl.BlockSpec((m, tn), lambda j: (0, j))],
        out_shape=[jax.ShapeDtypeStruct((m, D_FF), BF16), jax.ShapeDtypeStruct((m, D_FF), F32)],
        compiler_params=_params(1),
        name="ffn_up_dec",
    )(x, w_up, w_up, conv_w, conv_b, prev, prev)


def _layer(x, xb, h_prev, layer, p, *, dec=None):
    m = x.shape[0]
    t = _tiles(m)
    h = _proj(xb, p['w_in'], layer, n=IN_W, tm=t['proj'][0], tn=t['proj'][1],
              stack=(DEPTH, h_prev), name="in_proj")
    if dec is None:
        y_pool = _pool_seq(h, p['w_pool_grp'], p['pool_scale'], layer, tm=t['pool'])
        o_att = _attn_seq(h, layer)
    else:
        y_pool = _pool_dec(h, dec['pool'], p['w_pool_grp'], p['pool_scale'], layer)
        h4 = h.reshape(DEPTH, m, IN_W // HEAD_DIM, HEAD_DIM)
        o = _attn_dec(h4, dec['caches'], layer).reshape(DEC_BATCH, ATT_W)
        o_att = jnp.pad(o, ((0, m - DEC_BATCH), (0, 0))).astype(BF16)
    merged = _gate_merge(y_pool, o_att, h, p['w_br_pool'], p['w_br_att'], layer,
                         tm=t['gate'][0], tn=t['gate'][1])
    z = _proj(merged, p['w_out'], layer, n=D_MODEL, tm=t['proj'][0], tn=t['proj'][1],
              resid=x, alpha=ALPHA, name="out_proj")
    x1, x1b = _layer_norm(z, p['ln1_g'], p['ln1_b'], layer, tm=t['ln'])
    if dec is None:
        hm, tail = _ffn_up_seq(x1b, p['w_up'], p['conv_w'], p['conv_b'], layer,
                               tm=t['ffn_up'][0], tn=t['ffn_up'][1], n_chunks=t['ffn_chunks'])
    else:
        hm, tail = _ffn_up_dec(x1b, p['w_up'], p['conv_w'], p['conv_b'], layer, dec['conv'][layer],
                               tn=t['ffn_up'][1])
    z2 = _down(hm, p['w_down'], layer, x1, tm=t['down'][0], tn=t['down'][1], tk=t['down'][2])
    x2, x2b = _layer_norm(z2, p['ln2_g'], p['ln2_b'], layer, tm=t['ln'])
    return x2, x2b, h, tail


def _run_prompt(x_prompt, p):
    batch = x_prompt.shape[0]
    x = x_prompt.reshape(batch * SEQ, D_MODEL)
    xb = x.astype(BF16)
    h, tails = None, []
    for l in range(DEPTH):
        x, xb, h, tail = _layer(x, xb, h, l, p)
        tails.append(tail)
    h4 = h.reshape(DEPTH, batch, SEQ, IN_W)
    pool = h4[:, :, SEQ - POOL_CTX:, :POOL_W]
    kvs = []
    for g, (win, _) in enumerate(ATT_GROUPS):
        base = POOL_W + 3 * g * ATT_W + ATT_W
        keep = min(win, SEQ)
        kvs.append(h4[:, :, SEQ - keep:, base:base + 2 * ATT_W]
                   .reshape(DEPTH, batch, keep, 2, N_HEADS, HEAD_DIM))
    tm = _tiles(batch * SEQ)['ffn_up'][0]
    t5 = jnp.stack(tails).reshape(DEPTH, batch, SEQ // tm, 8, D_FF)
    conv = t5[:, :, SEQ // tm - 1, 6:8, :]
    return (x.reshape(batch, SEQ, D_MODEL), pool, *kvs, conv)


def _run_sample(x_sample, state_pool, caches_in, state_conv, p):
    pad = DEC_ROWS - DEC_BATCH
    x = jnp.pad(x_sample.reshape(DEC_BATCH, D_MODEL), ((0, pad), (0, 0)))
    dec = dict(
        pool=jnp.pad(jnp.transpose(state_pool, (0, 2, 1, 3)), ((0, 0), (0, 0), (0, pad), (0, 0))),
        conv=jnp.pad(jnp.transpose(state_conv, (0, 2, 1, 3)), ((0, 0), (0, 0), (0, pad), (0, 0))),
        caches=[c.reshape(DEPTH, DEC_BATCH, c.shape[2] // dil, dil, 2, N_HEADS, HEAD_DIM)
                for c, (_, dil) in zip(caches_in, ATT_GROUPS)],
    )
    xb = x.astype(BF16)
    h, a_new = None, []
    for l in range(DEPTH):
        x, xb, h, a = _layer(x, xb, h, l, p, dec=dec)
        a_new.append(a[:DEC_BATCH])
    h = h[:, :DEC_BATCH]
    pool = jnp.concatenate([state_pool[:, :, 1:], h[:, :, None, :POOL_W]], axis=2)
    kvs = []
    for g, (win, _) in enumerate(ATT_GROUPS):
        base = POOL_W + 3 * g * ATT_W + ATT_W
        kv_new = h[:, :, base:base + 2 * ATT_W].reshape(DEPTH, DEC_BATCH, 1, 2, N_HEADS, HEAD_DIM)
        keep = min(win, PAST_LEN + 1)
        kvs.append(jnp.concatenate([caches_in[g], kv_new], axis=2)[:, :, caches_in[g].shape[2] + 1 - keep:])
    conv = jnp.concatenate([state_conv[:, :, 1:], jnp.stack(a_new)[:, :, None, :]], axis=2)
    return (x[:DEC_BATCH].reshape(DEC_BATCH, 1, D_MODEL), pool, *kvs, conv)


def kernel(x_prompt, x_sample, state_pool, cache_kv1, cache_kv2, cache_kv3, state_conv, w_in, w_pool_grp, pool_scale, w_br_pool, w_br_att, w_out, ln1_g, ln1_b, w_up, conv_w, conv_b, w_down, ln2_g, ln2_b):
    def row3(t):
        return t.reshape(DEPTH, 1, t.shape[-1])

    p = dict(w_in=w_in, w_pool_grp=w_pool_grp, pool_scale=row3(pool_scale), w_br_pool=w_br_pool,
             w_br_att=w_br_att, w_out=w_out, ln1_g=row3(ln1_g), ln1_b=row3(ln1_b), w_up=w_up,
             conv_w=conv_w, conv_b=row3(conv_b), w_down=w_down, ln2_g=row3(ln2_g), ln2_b=row3(ln2_b))
    yp, pool_p, kv1_p, kv2_p, kv3_p, conv_p = _run_prompt(x_prompt, p)
    ys, pool_s, kv1_s, kv2_s, kv3_s, conv_s = _run_sample(
        x_sample, state_pool, (cache_kv1, cache_kv2, cache_kv3), state_conv, p)
    return (yp, ys, pool_p, pool_s, kv1_p, kv1_s, kv2_p, kv2_s, kv3_p, kv3_s, conv_p, conv_s)
```

```python
import functools

import jax
import jax.numpy as jnp
from jax import lax
from jax.experimental import pallas as pl
from jax.experimental.pallas import tpu as pltpu

D_MODEL = 4096
SEQ = 2048
DEPTH = 2
DEC_BATCH = 8
PAST_LEN = 16384
HEAD_DIM = 128
N_HEADS = 8
ATT_GROUPS = ((128, 1), (512, 4), (2048, 16))
N_GROUPS = len(ATT_GROUPS)
ATT_W = N_HEADS * HEAD_DIM
POOL_WINDOWS = (2, 4, 8, 16)
POOL_W = D_MODEL // 2
POOL_GC = POOL_W // len(POOL_WINDOWS)
POOL_CTX = max(POOL_WINDOWS) - 1
D_FF = 11008
GATE_OFF = POOL_W + 3 * N_GROUPS * ATT_W
IN_W = GATE_OFF + 2 * D_MODEL
ALPHA = (2.0 * DEPTH) ** 0.25
LN_EPS = 1e-5
NEG = -1e30
ATT_BLK = 128
SCALE = HEAD_DIM ** -0.5

DEC_ROWS = 16
HALO = 16
VMEM_LIMIT = 62 * 1024 * 1024

F32 = jnp.float32
BF16 = jnp.bfloat16

assert all(PAST_LEN >= win and win % dil == 0 for win, dil in ATT_GROUPS)


def _tiles(m):
    big = m >= 1024
    return dict(
        proj=(512 if big else m, 1024),
        gate=(512 if big else m, 1024),
        ffn_up=(2048 if big else m, 256),
        ffn_chunks=4,
        down=(2048, 1024, 1024) if big else (m, 1024, 2048),
        ln=256 if big else m,
        pool=512,
    )


def _params(n_axes, flags=None):
    return pltpu.CompilerParams(dimension_semantics=("arbitrary",) * n_axes,
                                vmem_limit_bytes=VMEM_LIMIT, flags=flags)


def _resid_operands(resid, tm, tn, at):
    if not isinstance(resid, tuple):
        return [pl.BlockSpec((tm, tn), lambda *g: at(*g))], [resid]
    z, mu, rstd, g3, b3, ln_layer = resid
    specs = [pl.BlockSpec((tm, tn), lambda *g: at(*g)),
             pl.BlockSpec((tm, 1), lambda *g: (at(*g)[0], 0)),
             pl.BlockSpec((tm, 1), lambda *g: (at(*g)[0], 0)),
             pl.BlockSpec((None, 1, tn), lambda *g: (ln_layer, 0, at(*g)[1])),
             pl.BlockSpec((None, 1, tn), lambda *g: (ln_layer, 0, at(*g)[1]))]
    return specs, [z, mu, rstd, g3, b3]


def _resid_value(r_refs):
    if len(r_refs) == 1:
        return r_refs[0][...]
    z_ref, mu_ref, rstd_ref, g_ref, b_ref = r_refs
    return (z_ref[...] - mu_ref[...]) * rstd_ref[...] * g_ref[...] + b_ref[...]


def _proj_kernel(*refs, alpha, n_resid, has_alias, emit_bf16, tn):
    refs = list(refs)
    x_ref, w_ref = refs[:2]
    r_refs = refs[2:2 + n_resid]
    n_in = 2 + n_resid + int(has_alias)
    o_ref = refs[n_in]
    wq_ref = refs[n_in + 1] if emit_bf16 else None
    wb_ref = refs[n_in + 1 + int(emit_bf16)]
    sem = refs[-1] if emit_bf16 else None
    j, i = pl.program_id(0), pl.program_id(1)

    def panel_copy(jj):
        return pltpu.make_async_copy(wb_ref, wq_ref.at[:, pl.ds(jj * tn, tn)], sem)

    @pl.when(i == 0)
    def _cast():
        if emit_bf16:
            @pl.when(j > 0)
            def _drain():
                panel_copy(j - 1).wait()
        wb_ref[...] = w_ref[...].astype(BF16)
        if emit_bf16:
            panel_copy(j).start()

    acc = jnp.dot(x_ref[...], wb_ref[...], preferred_element_type=F32)
    if n_resid:
        acc = alpha * _resid_value(r_refs) + acc
    if len(o_ref.shape) == 3:
        o_ref[0] = acc
        o_ref[1:] = jnp.zeros((o_ref.shape[0] - 1,) + acc.shape, o_ref.dtype)
    else:
        o_ref[...] = acc.astype(o_ref.dtype)

    if emit_bf16:
        @pl.when((j == pl.num_programs(0) - 1) & (i == pl.num_programs(1) - 1))
        def _last_wait():
            panel_copy(j).wait()


def _proj(x, w3, layer, *, n, tm, tn, resid=None, alpha=1.0, stack=None, emit_bf16=False, name):
    m, k = x.shape
    grid = (n // tn, m // tm)
    in_specs = [pl.BlockSpec((tm, k), lambda j, i: (i, 0)),
                pl.BlockSpec((None, k, tn), lambda j, i: (layer, 0, j))]
    args = [x, w3]
    n_resid = 0
    if resid is not None:
        r_specs, r_args = _resid_operands(resid, tm, tn, lambda j, i: (i, j))
        in_specs += r_specs
        args += r_args
        n_resid = len(r_args)
    aliases = {}
    if stack is None:
        out_shapes = [jax.ShapeDtypeStruct((m, n), F32)]
        out_specs = [pl.BlockSpec((tm, tn), lambda j, i: (i, j))]
    else:
        depth, prev = stack
        out_shapes = [jax.ShapeDtypeStruct((depth, m, n), F32)]
        out_specs = [pl.BlockSpec((None, tm, tn), lambda j, i: (layer, i, j))]
        if prev is None:
            assert layer == 0
            out_specs = [pl.BlockSpec((depth, tm, tn), lambda j, i: (0, i, j))]
        else:
            in_specs.append(pl.BlockSpec(memory_space=pl.ANY))
            aliases = {len(args): 0}
            args.append(prev)
    scratch = [pltpu.VMEM((k, tn), BF16)]
    if emit_bf16:
        out_shapes.append(jax.ShapeDtypeStruct((k, n), BF16))
        out_specs.append(pl.BlockSpec(memory_space=pl.ANY))
        scratch.append(pltpu.SemaphoreType.DMA(()))
    outs = pl.pallas_call(
        functools.partial(_proj_kernel, alpha=alpha, n_resid=n_resid, has_alias=bool(aliases),
                          emit_bf16=emit_bf16, tn=tn),
        grid=grid,
        in_specs=in_specs,
        out_specs=out_specs,
        out_shape=out_shapes,
        scratch_shapes=scratch,
        input_output_aliases=aliases,
        compiler_params=_params(2),
        name=name,
    )(*args)
    return tuple(outs) if emit_bf16 else outs[0]


def _proj_q(x, wq, *, tn, layer, stack, name):
    m, k = x.shape
    n = wq.shape[1]
    depth, prev = stack
    in_specs = [pl.BlockSpec((m, k), lambda j: (0, 0)), pl.BlockSpec((k, tn), lambda j: (0, j))]
    args = [x, wq]
    aliases = {}
    if prev is None:
        assert layer == 0
        out_spec = pl.BlockSpec((depth, m, tn), lambda j: (0, 0, j))
        kern = _proj_q_first_kernel
    else:
        out_spec = pl.BlockSpec((None, m, tn), lambda j: (layer, 0, j))
        in_specs.append(pl.BlockSpec(memory_space=pl.ANY))
        aliases = {2: 0}
        args.append(prev)
        kern = _proj_q_next_kernel
    return pl.pallas_call(
        kern,
        grid=(n // tn,),
        in_specs=in_specs,
        out_specs=out_spec,
        out_shape=jax.ShapeDtypeStruct((depth, m, n), F32),
        input_output_aliases=aliases,
        compiler_params=_params(1),
        name=name,
    )(*args)


def _proj_q_first_kernel(x_ref, w_ref, o_ref):
    o_ref[0] = jnp.dot(x_ref[...], w_ref[...], preferred_element_type=F32)
    o_ref[1:] = jnp.zeros((o_ref.shape[0] - 1,) + o_ref.shape[1:], o_ref.dtype)


def _proj_q_next_kernel(x_ref, w_ref, prev_ref, o_ref):
    del prev_ref
    o_ref[...] = jnp.dot(x_ref[...], w_ref[...], preferred_element_type=F32)


def _down_kernel(*refs, alpha, rem):
    x_ref, w_ref = refs[:2]
    r_refs = refs[2:-1]
    o_ref = refs[-1]
    kk = pl.program_id(2)
    nk = pl.num_programs(2)

    @pl.when(kk == 0)
    def _first():
        o_ref[...] = alpha * _resid_value(r_refs) + jnp.dot(x_ref[...], w_ref[...].astype(BF16),
                                                            preferred_element_type=F32)

    @pl.when((kk > 0) & (kk < nk - 1))
    def _full():
        o_ref[...] += jnp.dot(x_ref[...], w_ref[...].astype(BF16), preferred_element_type=F32)

    @pl.when(kk == nk - 1)
    def _last():
        o_ref[...] += jnp.dot(x_ref[:, :rem], w_ref[:rem, :].astype(BF16), preferred_element_type=F32)


def _down(x, w3, layer, resid, *, tm, tn, tk):
    m, k = x.shape
    n = w3.shape[2]
    nk = pl.cdiv(k, tk)
    assert nk >= 2
    rem = k - (nk - 1) * tk
    r_specs, r_args = _resid_operands(resid, tm, tn, lambda j, i, kk: (i, j))
    return pl.pallas_call(
        functools.partial(_down_kernel, alpha=ALPHA, rem=rem),
        grid=(n // tn, m // tm, nk),
        in_specs=[pl.BlockSpec((tm, tk), lambda j, i, kk: (i, kk)),
                  pl.BlockSpec((None, tk, tn), lambda j, i, kk: (layer, kk, j))] + r_specs,
        out_specs=pl.BlockSpec((tm, tn), lambda j, i, kk: (i, j)),
        out_shape=jax.ShapeDtypeStruct((m, n), F32),
        compiler_params=_params(3),
        name="down_proj",
    )(x, w3, *r_args)


def _ln_kernel(z_ref, g_ref, b_ref, *o_refs, full):
    z = z_ref[...]
    mu = jnp.mean(z, axis=-1, keepdims=True)
    zc = z - mu
    var = jnp.mean(zc * zc, axis=-1, keepdims=True)
    rstd = lax.rsqrt(var + LN_EPS)
    y = zc * rstd * g_ref[...] + b_ref[...]
    if full:
        o_refs[0][...] = y
    else:
        ob_ref, mu_ref, rstd_ref = o_refs
        ob_ref[...] = y.astype(BF16)
        mu_ref[...] = mu
        rstd_ref[...] = rstd


def _layer_norm(z, g3, b3, layer, *, tm, full=False):
    m, d = z.shape
    row = pl.BlockSpec((tm, d), lambda i: (i, 0))
    col = pl.BlockSpec((tm, 1), lambda i: (i, 0))
    if full:
        out_specs, out_shape = [row], [jax.ShapeDtypeStruct((m, d), F32)]
    else:
        out_specs = [row, col, col]
        out_shape = [jax.ShapeDtypeStruct((m, d), BF16), jax.ShapeDtypeStruct((m, 1), F32),
                     jax.ShapeDtypeStruct((m, 1), F32)]
    outs = pl.pallas_call(
        functools.partial(_ln_kernel, full=full),
        grid=(m // tm,),
        in_specs=[row,
                  pl.BlockSpec((None, 1, d), lambda i: (layer, 0, 0)),
                  pl.BlockSpec((None, 1, d), lambda i: (layer, 0, 0))],
        out_specs=out_specs,
        out_shape=out_shape,
        compiler_params=_params(1),
        name="layer_norm",
    )(z, g3, b3)
    return outs[0] if full else tuple(outs)


def _pool_groups(window_sum_fn, u_fn, inv_cnt_fn, wb_ref, sc_ref, y_ref):
    for g, w in enumerate(POOL_WINDOWS):
        cols = slice(g * POOL_GC, (g + 1) * POOL_GC)
        d = window_sum_fn(g, w, cols) * inv_cnt_fn(w) - u_fn(cols)
        y = jnp.dot(d.astype(BF16), wb_ref[g], preferred_element_type=F32) * sc_ref[:, cols]
        y_ref[:, cols] = y.astype(y_ref.dtype)


def _pool_seq_kernel(u_ref, halo_ref, wg_ref, sc_ref, y_ref, wb_ref, *, tm, tiles_per_seq):
    i = pl.program_id(0)

    @pl.when(i == 0)
    def _cast():
        wb_ref[...] = wg_ref[...].astype(BF16)

    t_in_seq = i % tiles_per_seq
    halo = jnp.where(t_in_seq == 0, 0.0, halo_ref[...])
    pos = t_in_seq * tm + lax.broadcasted_iota(jnp.int32, (tm, 1), 0)

    def window_sum(g, w, cols):
        s = jnp.concatenate([halo[:, cols], u_ref[:, cols]], axis=0)
        k = 1
        while k < w:
            s = s + pltpu.roll(s, k, 0)
            k *= 2
        return s[HALO:]

    def inv_cnt(w):
        return 1.0 / jnp.minimum(pos + 1, w).astype(F32)

    _pool_groups(window_sum, lambda cols: u_ref[:, cols], inv_cnt, wb_ref, sc_ref, y_ref)


def _pool_seq(h, w_pool_grp, pool_scale, layer, *, tm):
    m = h.shape[1]
    tiles_per_seq = SEQ // tm
    return pl.pallas_call(
        functools.partial(_pool_seq_kernel, tm=tm, tiles_per_seq=tiles_per_seq),
        grid=(m // tm,),
        in_specs=[pl.BlockSpec((None, tm, POOL_W), lambda i: (layer, i, 0)),
                  pl.BlockSpec((None, HALO, POOL_W),
                               lambda i: (layer, jnp.maximum(i * (tm // HALO) - 1, 0), 0)),
                  pl.BlockSpec((None, len(POOL_WINDOWS), POOL_GC, POOL_GC), lambda i: (layer, 0, 0, 0)),
                  pl.BlockSpec((None, 1, POOL_W), lambda i: (layer, 0, 0))],
        out_specs=pl.BlockSpec((tm, POOL_W), lambda i: (i, 0)),
        out_shape=jax.ShapeDtypeStruct((m, POOL_W), BF16),
        scratch_shapes=[pltpu.VMEM((len(POOL_WINDOWS), POOL_GC, POOL_GC), BF16)],
        compiler_params=_params(1),
        name="pool_seq",
    )(h, h, w_pool_grp, pool_scale)


def _pool_dec_kernel(u_ref, st_ref, wg_ref, sc_ref, y_ref, wb_ref):
    wb_ref[...] = wg_ref[...].astype(BF16)

    def window_sum(g, w, cols):
        s = u_ref[:, cols]
        for k in range(1, w):
            s = s + st_ref[POOL_CTX - k, :, cols]
        return s

    def inv_cnt(w):
        return 1.0 / float(min(PAST_LEN + 1, w))

    _pool_groups(window_sum, lambda cols: u_ref[:, cols], inv_cnt, wb_ref, sc_ref, y_ref)


def _pool_dec(h, st, w_pool_grp, pool_scale, layer):
    m = h.shape[1]
    return pl.pallas_call(
        _pool_dec_kernel,
        grid=(1,),
        in_specs=[pl.BlockSpec((None, m, POOL_W), lambda i: (layer, 0, 0)),
                  pl.BlockSpec((None, POOL_CTX, m, POOL_W), lambda i: (layer, 0, 0, 0)),
                  pl.BlockSpec((None, len(POOL_WINDOWS), POOL_GC, POOL_GC), lambda i: (layer, 0, 0, 0)),
                  pl.BlockSpec((None, 1, POOL_W), lambda i: (layer, 0, 0))],
        out_specs=pl.BlockSpec((m, POOL_W), lambda i: (0, 0)),
        out_shape=jax.ShapeDtypeStruct((m, POOL_W), BF16),
        scratch_shapes=[pltpu.VMEM((len(POOL_WINDOWS), POOL_GC, POOL_GC), BF16)],
        compiler_params=_params(1),
        name="pool_dec",
    )(h, st, w_pool_grp, pool_scale)


def _merge_groups(o_list, lse_list):
    mx = functools.reduce(jnp.maximum, lse_list)
    ws = [jnp.exp(l - mx) for l in lse_list]
    num = functools.reduce(lambda a, b: a + b, [w * o for w, o in zip(ws, o_list)])
    den = functools.reduce(lambda a, b: a + b, ws)
    return num / den


def _attn_seq_kernel(*refs):
    qkv = refs[:9]
    o_ref = refs[9]
    og_refs = refs[10:13]
    lg_refs = refs[13:16]

    qi = lax.broadcasted_iota(jnp.int32, (ATT_BLK, ATT_BLK), 0)
    kj = lax.broadcasted_iota(jnp.int32, (ATT_BLK, ATT_BLK), 1)
    cur_mask = kj <= qi
    qi2 = lax.broadcasted_iota(jnp.int32, (ATT_BLK, 2 * ATT_BLK), 0)
    kj2 = lax.broadcasted_iota(jnp.int32, (ATT_BLK, 2 * ATT_BLK), 1)
    both_mask = (kj2 >= qi2) & (kj2 - qi2 <= ATT_BLK)

    for g, (win, dil) in enumerate(ATT_GROUPS):
        q_ref, k_ref, v_ref = qkv[3 * g:3 * g + 3]
        n_blk = SEQ // dil // ATT_BLK
        for r in range(dil):
            for nb in range(n_blk):
                q_rows = pl.ds(r + nb * ATT_BLK * dil, ATT_BLK, stride=dil)
                if nb == 0:
                    kv_rows, mask = q_rows, cur_mask
                else:
                    kv_rows = pl.ds(r + (nb - 1) * ATT_BLK * dil, 2 * ATT_BLK, stride=dil)
                    mask = both_mask
                q = q_ref[q_rows, :].astype(BF16)
                k = k_ref[kv_rows, :].astype(BF16)
                v = v_ref[kv_rows, :].astype(BF16)
                s = lax.dot_general(q, k, (((1,), (1,)), ((), ())), preferred_element_type=F32) * SCALE
                s = jnp.where(mask, s, NEG)
                mx = jnp.max(s, axis=-1, keepdims=True)
                p = jnp.exp(s - mx)
                den = jnp.sum(p, axis=-1, keepdims=True)
                o = jnp.dot(p.astype(BF16), v, preferred_element_type=F32) / den
                og_refs[g][q_rows, :] = o
                lg_refs[g][q_rows, :] = jnp.broadcast_to(mx + jnp.log(den), (ATT_BLK, HEAD_DIM))

    chunk = 256
    for c in range(SEQ // chunk):
        rows = pl.ds(c * chunk, chunk)
        out = _merge_groups([r_[rows, :] for r_ in og_refs], [r_[rows, :] for r_ in lg_refs])
        o_ref[rows, :] = out.astype(o_ref.dtype)


def _attn_seq(h, layer):
    m = h.shape[1]
    in_specs = []
    for g in range(N_GROUPS):
        for which in range(3):
            blk0 = (POOL_W + (3 * g + which) * ATT_W) // HEAD_DIM
            in_specs.append(pl.BlockSpec((None, SEQ, HEAD_DIM),
                                         lambda b, hd, blk0=blk0: (layer, b, blk0 + hd)))
    return pl.pallas_call(
        _attn_seq_kernel,
        grid=(m // SEQ, N_HEADS),
        in_specs=in_specs,
        out_specs=pl.BlockSpec((SEQ, HEAD_DIM), lambda b, hd: (b, hd)),
        out_shape=jax.ShapeDtypeStruct((m, ATT_W), BF16),
        scratch_shapes=[pltpu.VMEM((SEQ, HEAD_DIM), F32)] * (2 * N_GROUPS),
        compiler_params=_params(2),
        name="attn_seq",
    )(*([h] * 9))


def _attn_dec_kernel(h_ref, c1_ref, c2_ref, c3_ref, o_ref):
    def rnd(t):
        return t.astype(BF16).astype(F32)

    outs, lses = [], []
    for g, c_ref in enumerate((c1_ref, c2_ref, c3_ref)):
        r0 = (POOL_W + 3 * g * ATT_W) // HEAD_DIM
        q = rnd(h_ref[r0:r0 + N_HEADS, :])
        k_new = rnd(h_ref[r0 + N_HEADS:r0 + 2 * N_HEADS, :])
        v_new = rnd(h_ref[r0 + 2 * N_HEADS:r0 + 3 * N_HEADS, :])
        kc = rnd(c_ref[:, 0])
        vc = rnd(c_ref[:, 1])
        s = jnp.sum(kc * q[None], axis=-1, keepdims=True) * SCALE
        s_new = jnp.sum(k_new * q, axis=-1, keepdims=True) * SCALE
        mx = jnp.maximum(jnp.max(s, axis=0), s_new)
        p = jnp.exp(s - mx[None])
        p_new = jnp.exp(s_new - mx)
        den = jnp.sum(p, axis=0) + p_new
        acc = jnp.sum(rnd(p) * vc, axis=0) + rnd(p_new) * v_new
        outs.append(acc / den)
        lses.append(mx + jnp.log(den))
    o_ref[...] = _merge_groups(outs, lses)


def _attn_dec(h4, caches, layer):
    in_specs = [pl.BlockSpec((None, None, IN_W // HEAD_DIM, HEAD_DIM), lambda b: (layer, b, 0, 0))]
    for c in caches:
        in_specs.append(pl.BlockSpec((None, None, c.shape[2], None, 2, N_HEADS, HEAD_DIM),
                                     lambda b: (layer, b, 0, 0, 0, 0, 0)))
    return pl.pallas_call(
        _attn_dec_kernel,
        grid=(DEC_BATCH,),
        in_specs=in_specs,
        out_specs=pl.BlockSpec((None, N_HEADS, HEAD_DIM), lambda b: (b, 0, 0)),
        out_shape=jax.ShapeDtypeStruct((DEC_BATCH, N_HEADS, HEAD_DIM), F32),
        compiler_params=_params(1),
        name="attn_dec",
    )(h4, *caches)


def _gate_kernel(yp_ref, oa_ref, wp_ref, wa_ref, gp_ref, ga_ref, o_ref, wpb_ref, wab_ref):
    @pl.when(pl.program_id(1) == 0)
    def _cast():
        wpb_ref[...] = wp_ref[...].astype(BF16)
        wab_ref[...] = wa_ref[...].astype(BF16)

    bp = jnp.dot(yp_ref[...], wpb_ref[...], preferred_element_type=F32)
    ba = jnp.dot(oa_ref[...], wab_ref[...], preferred_element_type=F32)
    merged = jax.nn.sigmoid(gp_ref[...]) * bp + jax.nn.sigmoid(ga_ref[...]) * ba
    o_ref[...] = merged.astype(o_ref.dtype)


def _gate_merge(y_pool, o_att, h, w_br_pool, w_br_att, layer, *, tm, tn):
    m = h.shape[1]
    gp0 = GATE_OFF // tn
    ga0 = (GATE_OFF + D_MODEL) // tn
    return pl.pallas_call(
        _gate_kernel,
        grid=(D_MODEL // tn, m // tm),
        in_specs=[pl.BlockSpec((tm, POOL_W), lambda j, i: (i, 0)),
                  pl.BlockSpec((tm, ATT_W), lambda j, i: (i, 0)),
                  pl.BlockSpec((None, POOL_W, tn), lambda j, i: (layer, 0, j)),
                  pl.BlockSpec((None, ATT_W, tn), lambda j, i: (layer, 0, j)),
                  pl.BlockSpec((None, tm, tn), lambda j, i: (layer, i, gp0 + j)),
                  pl.BlockSpec((None, tm, tn), lambda j, i: (layer, i, ga0 + j))],
        out_specs=pl.BlockSpec((tm, tn), lambda j, i: (i, j)),
        out_shape=jax.ShapeDtypeStruct((m, D_MODEL), BF16),
        scratch_shapes=[pltpu.VMEM((POOL_W, tn), BF16), pltpu.VMEM((ATT_W, tn), BF16)],
        compiler_params=_params(2),
        name="gate_merge",
    )(y_pool, o_att, w_br_pool, w_br_att, h, h)


def _gelu_gate(a2, a1, a, b, cw_ref, cb_ref):
    ac = cw_ref[0:1] * a2 + cw_ref[1:2] * a1 + cw_ref[2:3] * a + cb_ref[...]
    gelu = 0.5 * ac * (1.0 + lax.erf(ac * (0.5 ** 0.5)))
    return gelu * b


def _ffn_up_seq_kernel(x_ref, wa_ref, wb_ref, cw_ref, cb_ref, hm_ref, tail_ref, wq_ref,
                       wab_ref, wbb_ref, carry_ref, sem, *, tm, tn, tiles_per_seq, n_chunks):
    j, i = pl.program_id(0), pl.program_id(1)
    n_panels = pl.num_programs(0)

    def panel_copies(jj):
        return (pltpu.make_async_copy(wab_ref, wq_ref.at[:, pl.ds(jj * tn, tn)], sem.at[0]),
                pltpu.make_async_copy(wbb_ref, wq_ref.at[:, pl.ds((n_panels + jj) * tn, tn)], sem.at[1]))

    @pl.when(i == 0)
    def _cast():
        @pl.when(j > 0)
        def _drain():
            for c in panel_copies(j - 1):
                c.wait()
        wab_ref[...] = wa_ref[...].astype(BF16)
        wbb_ref[...] = wb_ref[...].astype(BF16)
        for c in panel_copies(j):
            c.start()

    @pl.when(i % tiles_per_seq == 0)
    def _reset():
        carry_ref[...] = jnp.zeros_like(carry_ref)

    tc = tm // n_chunks
    row = lax.broadcasted_iota(jnp.int32, (tc, 1), 0)
    carry = carry_ref[...]
    for c in range(n_chunks):
        rows = pl.ds(c * tc, tc)
        x = x_ref[rows, :]
        a = jnp.dot(x, wab_ref[...], preferred_element_type=F32)
        b = jnp.dot(x, wbb_ref[...], preferred_element_type=F32)
        a1 = jnp.where(row == 0, carry[7:8], pltpu.roll(a, 1, 0))
        a2 = jnp.where(row == 0, carry[6:7], jnp.where(row == 1, carry[7:8], pltpu.roll(a, 2, 0)))
        hm_ref[rows, :] = _gelu_gate(a2, a1, a, b, cw_ref, cb_ref).astype(hm_ref.dtype)
        carry = a[tc - 8:]
    carry_ref[...] = carry
    tail_ref[0] = carry

    @pl.when((j == n_panels - 1) & (i == pl.num_programs(1) - 1))
    def _last_wait():
        for c in panel_copies(j):
            c.wait()


def _ffn_up_seq(x, w_up, conv_w, conv_b, layer, *, tm, tn, n_chunks):
    m, k = x.shape
    n_panels = D_FF // tn
    return pl.pallas_call(
        functools.partial(_ffn_up_seq_kernel, tm=tm, tn=tn, tiles_per_seq=SEQ // tm, n_chunks=n_chunks),
        grid=(n_panels, m // tm),
        in_specs=[pl.BlockSpec((tm, k), lambda j, i: (i, 0)),
                  pl.BlockSpec((None, k, tn), lambda j, i: (layer, 0, j)),
                  pl.BlockSpec((None, k, tn), lambda j, i: (layer, 0, n_panels + j)),
                  pl.BlockSpec((None, 3, tn), lambda j, i: (layer, 0, j)),
                  pl.BlockSpec((None, 1, tn), lambda j, i: (layer, 0, j))],
        out_specs=[pl.BlockSpec((tm, tn), lambda j, i: (i, j)),
                   pl.BlockSpec((1, 8, tn), lambda j, i: (i, 0, j)),
                   pl.BlockSpec(memory_space=pl.ANY)],
        out_shape=[jax.ShapeDtypeStruct((m, D_FF), BF16),
                   jax.ShapeDtypeStruct((m // tm, 8, D_FF), F32),
                   jax.ShapeDtypeStruct((k, 2 * D_FF), BF16)],
        scratch_shapes=[pltpu.VMEM((k, tn), BF16), pltpu.VMEM((k, tn), BF16),
                        pltpu.VMEM((8, tn), F32), pltpu.SemaphoreType.DMA((2,))],
        compiler_params=_params(2),
        name="ffn_up",
    )(x, w_up, w_up, conv_w, conv_b)


def _ffn_up_dec_kernel(x_ref, wa_ref, wb_ref, cw_ref, cb_ref, p2_ref, p1_ref, hm_ref, a_ref):
    x = x_ref[...]
    a = jnp.dot(x, wa_ref[...], preferred_element_type=F32)
    b = jnp.dot(x, wb_ref[...], preferred_element_type=F32)
    a_ref[...] = a
    hm_ref[...] = _gelu_gate(p2_ref[...], p1_ref[...], a, b, cw_ref, cb_ref).astype(hm_ref.dtype)


def _ffn_up_dec(x, wq_up, conv_w, conv_b, layer, prev, *, tn):
    m, k = x.shape
    n_panels = D_FF // tn
    return pl.pallas_call(
        _ffn_up_dec_kernel,
        grid=(n_panels,),
        in_specs=[pl.BlockSpec((m, k), lambda j: (0, 0)),
                  pl.BlockSpec((k, tn), lambda j: (0, j)),
                  pl.BlockSpec((k, tn), lambda j: (0, n_panels + j)),
                  pl.BlockSpec((None, 3, tn), lambda j: (layer, 0, j)),
                  pl.BlockSpec((None, 1, tn), lambda j: (layer, 0, j)),
                  pl.BlockSpec((None, m, tn), lambda j: (0, 0, j)),
                  pl.BlockSpec((None, m, tn), lambda j: (1, 0, j))],
        out_specs=[pl.BlockSpec((m, tn), lambda j: (0, j)), pl.BlockSpec((m, tn), lambda j: (0, j))],
        out_shape=[jax.ShapeDtypeStruct((m, D_FF), BF16), jax.ShapeDtypeStruct((m, D_FF), F32)],
        compiler_params=_params(1),
        name="ffn_up_dec",
    )(x, wq_up, wq_up, conv_w, conv_b, prev, prev)


def _layer(resid, xb, h_prev, layer, p, *, dec=None):
    m = xb.shape[0]
    t = _tiles(m)
    out = {}
    if dec is None:
        h, out['wq_in'] = _proj(xb, p['w_in'], layer, n=IN_W, tm=t['proj'][0], tn=t['proj'][1],
                                stack=(DEPTH, h_prev), emit_bf16=True, name="in_proj")
    else:
        h = _proj_q(xb, dec['wq_in'][layer], tn=t['proj'][1], layer=layer, stack=(DEPTH, h_prev),
                    name="in_proj_dec")
    if dec is None:
        y_pool = _pool_seq(h, p['w_pool_grp'], p['pool_scale'], layer, tm=t['pool'])
        o_att = _attn_seq(h, layer)
    else:
        y_pool = _pool_dec(h, dec['pool'], p['w_pool_grp'], p['pool_scale'], layer)
        h4 = h.reshape(DEPTH, m, IN_W // HEAD_DIM, HEAD_DIM)
        o = _attn_dec(h4, dec['caches'], layer).reshape(DEC_BATCH, ATT_W)
        o_att = jnp.pad(o, ((0, m - DEC_BATCH), (0, 0))).astype(BF16)
    merged = _gate_merge(y_pool, o_att, h, p['w_br_pool'], p['w_br_att'], layer,
                         tm=t['gate'][0], tn=t['gate'][1])
    z = _proj(merged, p['w_out'], layer, n=D_MODEL, tm=t['proj'][0], tn=t['proj'][1],
              resid=resid, alpha=ALPHA, name="out_proj")
    x1b, mu1, rstd1 = _layer_norm(z, p['ln1_g'], p['ln1_b'], layer, tm=t['ln'])
    if dec is None:
        hm, tail, out['wq_up'] = _ffn_up_seq(x1b, p['w_up'], p['conv_w'], p['conv_b'], layer,
                                             tm=t['ffn_up'][0], tn=t['ffn_up'][1],
                                             n_chunks=t['ffn_chunks'])
    else:
        hm, tail = _ffn_up_dec(x1b, dec['wq_up'][layer], p['conv_w'], p['conv_b'], layer,
                               dec['conv'][layer], tn=t['ffn_up'][1])
    z2 = _down(hm, p['w_down'], layer, (z, mu1, rstd1, p['ln1_g'], p['ln1_b'], layer),
               tm=t['down'][0], tn=t['down'][1], tk=t['down'][2])
    if layer == DEPTH - 1:
        out['y'] = _layer_norm(z2, p['ln2_g'], p['ln2_b'], layer, tm=t['ln'], full=True)
    else:
        out['xb'], mu2, rstd2 = _layer_norm(z2, p['ln2_g'], p['ln2_b'], layer, tm=t['ln'])
        out['resid'] = (z2, mu2, rstd2, p['ln2_g'], p['ln2_b'], layer)
    out.update(h=h, tail=tail)
    return out


def _run_prompt(x_prompt, p):
    batch = x_prompt.shape[0]
    resid = x_prompt.reshape(batch * SEQ, D_MODEL)
    xb = resid.astype(BF16)
    h, tails, wq_in, wq_up = None, [], [], []
    for l in range(DEPTH):
        o = _layer(resid, xb, h, l, p)
        h, resid, xb = o['h'], o.get('resid'), o.get('xb')
        tails.append(o['tail'])
        wq_in.append(o['wq_in'])
        wq_up.append(o['wq_up'])
    x = o['y']
    h4 = h.reshape(DEPTH, batch, SEQ, IN_W)
    pool = h4[:, :, SEQ - POOL_CTX:, :POOL_W]
    kvs = []
    for g, (win, _) in enumerate(ATT_GROUPS):
        base = POOL_W + 3 * g * ATT_W + ATT_W
        keep = min(win, SEQ)
        kvs.append(h4[:, :, SEQ - keep:, base:base + 2 * ATT_W]
                   .reshape(DEPTH, batch, keep, 2, N_HEADS, HEAD_DIM))
    tm = _tiles(batch * SEQ)['ffn_up'][0]
    t5 = jnp.stack(tails).reshape(DEPTH, batch, SEQ // tm, 8, D_FF)
    conv = t5[:, :, SEQ // tm - 1, 6:8, :]
    return (x.reshape(batch, SEQ, D_MODEL), pool, *kvs, conv), wq_in, wq_up


def _run_sample(x_sample, state_pool, caches_in, state_conv, p, wq_in, wq_up):
    pad = DEC_ROWS - DEC_BATCH
    resid = jnp.pad(x_sample.reshape(DEC_BATCH, D_MODEL), ((0, pad), (0, 0)))
    dec = dict(
        wq_in=wq_in, wq_up=wq_up,
        pool=jnp.pad(jnp.transpose(state_pool, (0, 2, 1, 3)), ((0, 0), (0, 0), (0, pad), (0, 0))),
        conv=jnp.pad(jnp.transpose(state_conv, (0, 2, 1, 3)), ((0, 0), (0, 0), (0, pad), (0, 0))),
        caches=[c.reshape(DEPTH, DEC_BATCH, c.shape[2] // dil, dil, 2, N_HEADS, HEAD_DIM)
                for c, (_, dil) in zip(caches_in, ATT_GROUPS)],
    )
    xb = resid.astype(BF16)
    h, a_new = None, []
    for l in range(DEPTH):
        o = _layer(resid, xb, h, l, p, dec=dec)
        h, resid, xb = o['h'], o.get('resid'), o.get('xb')
        a_new.append(o['tail'][:DEC_BATCH])
    x = o['y']
    h = h[:, :DEC_BATCH]
    pool = jnp.concatenate([state_pool[:, :, 1:], h[:, :, None, :POOL_W]], axis=2)
    kvs = []
    for g, (win, _) in enumerate(ATT_GROUPS):
        base = POOL_W + 3 * g * ATT_W + ATT_W
        kv_new = h[:, :, base:base + 2 * ATT_W].reshape(DEPTH, DEC_BATCH, 1, 2, N_HEADS, HEAD_DIM)
        keep = min(win, PAST_LEN + 1)
        kvs.append(jnp.concatenate([caches_in[g], kv_new], axis=2)[:, :, caches_in[g].shape[2] + 1 - keep:])
    conv = jnp.concatenate([state_conv[:, :, 1:], jnp.stack(a_new)[:, :, None, :]], axis=2)
    return (x[:DEC_BATCH].reshape(DEC_BATCH, 1, D_MODEL), pool, *kvs, conv)


def kernel(x_prompt, x_sample, state_pool, cache_kv1, cache_kv2, cache_kv3, state_conv, w_in, w_pool_grp, pool_scale, w_br_pool, w_br_att, w_out, ln1_g, ln1_b, w_up, conv_w, conv_b, w_down, ln2_g, ln2_b):
    def row3(t):
        return t.reshape(DEPTH, 1, t.shape[-1])

    p = dict(w_in=w_in, w_pool_grp=w_pool_grp, pool_scale=row3(pool_scale), w_br_pool=w_br_pool,
             w_br_att=w_br_att, w_out=w_out, ln1_g=row3(ln1_g), ln1_b=row3(ln1_b), w_up=w_up,
             conv_w=conv_w, conv_b=row3(conv_b), w_down=w_down, ln2_g=row3(ln2_g), ln2_b=row3(ln2_b))
    (yp, pool_p, kv1_p, kv2_p, kv3_p, conv_p), wq_in, wq_up = _run_prompt(x_prompt, p)
    ys, pool_s, kv1_s, kv2_s, kv3_s, conv_s = _run_sample(
        x_sample, state_pool, (cache_kv1, cache_kv2, cache_kv3), state_conv, p, wq_in, wq_up)
    return (yp, ys, pool_p, pool_s, kv1_p, kv1_s, kv2_p, kv2_s, kv3_p, kv3_s, conv_p, conv_s)
```

```python
import functools

import jax
import jax.numpy as jnp
from jax import lax
from jax.experimental import pallas as pl
from jax.experimental.pallas import tpu as pltpu

D_MODEL = 4096
SEQ = 2048
DEPTH = 2
DEC_BATCH = 8
PAST_LEN = 16384
HEAD_DIM = 128
N_HEADS = 8
ATT_GROUPS = ((128, 1), (512, 4), (2048, 16))
N_GROUPS = len(ATT_GROUPS)
ATT_W = N_HEADS * HEAD_DIM
POOL_WINDOWS = (2, 4, 8, 16)
POOL_W = D_MODEL // 2
POOL_GC = POOL_W // len(POOL_WINDOWS)
POOL_CTX = max(POOL_WINDOWS) - 1
D_FF = 11008
GATE_OFF = POOL_W + 3 * N_GROUPS * ATT_W
IN_W = GATE_OFF + 2 * D_MODEL
ALPHA = (2.0 * DEPTH) ** 0.25
LN_EPS = 1e-5
NEG = -1e30
ATT_BLK = 128
SCALE = HEAD_DIM ** -0.5

LANES = 128
DEC_ROWS = 16
HALO = 16
VMEM_LIMIT = 62 * 1024 * 1024

F32 = jnp.float32
BF16 = jnp.bfloat16

assert all(PAST_LEN >= win and win % dil == 0 for win, dil in ATT_GROUPS)


def _tiles(m):
    big = m >= 1024
    return dict(
        proj=(512 if big else m, 1024),
        gate=(512 if big else m, 1024),
        ffn_up=(2048 if big else m, 256),
        ffn_chunks=4,
        down=(2048, 1024, 1024) if big else (m, 1024, 2048),
        ln=512 if big else m,
        pool=512,
    )


def _params(n_axes, flags=None):
    return pltpu.CompilerParams(dimension_semantics=("arbitrary",) * n_axes,
                                vmem_limit_bytes=VMEM_LIMIT, flags=flags)


def _resid_operands(resid, tm, tn, at):
    if not isinstance(resid, tuple):
        return [pl.BlockSpec((tm, tn), lambda *g: at(*g))], [resid]
    z, mu, rstd, g3, b3, ln_layer = resid
    specs = [pl.BlockSpec((tm, tn), lambda *g: at(*g)),
             pl.BlockSpec((tm, LANES), lambda *g: (at(*g)[0], 0)),
             pl.BlockSpec((tm, LANES), lambda *g: (at(*g)[0], 0)),
             pl.BlockSpec((None, 1, tn), lambda *g: (ln_layer, 0, at(*g)[1])),
             pl.BlockSpec((None, 1, tn), lambda *g: (ln_layer, 0, at(*g)[1]))]
    return specs, [z, mu, rstd, g3, b3]


def _resid_value(r_refs):
    if len(r_refs) == 1:
        return r_refs[0][...]
    z_ref, mu_ref, rstd_ref, g_ref, b_ref = r_refs
    mu, rstd = mu_ref[...], rstd_ref[...]
    parts = []
    for c in range(z_ref.shape[1] // LANES):
        cols = slice(c * LANES, (c + 1) * LANES)
        parts.append((z_ref[:, cols] - mu) * rstd * g_ref[:, cols] + b_ref[:, cols])
    return jnp.concatenate(parts, axis=1)


def _proj_kernel(*refs, alpha, n_resid, has_alias, emit_bf16, tn):
    refs = list(refs)
    x_ref, w_ref = refs[:2]
    r_refs = refs[2:2 + n_resid]
    n_in = 2 + n_resid + int(has_alias)
    o_ref = refs[n_in]
    wq_ref = refs[n_in + 1] if emit_bf16 else None
    wb_ref = refs[n_in + 1 + int(emit_bf16)]
    sem = refs[-1] if emit_bf16 else None
    j, i = pl.program_id(0), pl.program_id(1)

    def panel_copy(jj):
        return pltpu.make_async_copy(wb_ref, wq_ref.at[:, pl.ds(jj * tn, tn)], sem)

    @pl.when(i == 0)
    def _cast():
        if emit_bf16:
            @pl.when(j > 0)
            def _drain():
                panel_copy(j - 1).wait()
        wb_ref[...] = w_ref[...].astype(BF16)
        if emit_bf16:
            panel_copy(j).start()

    acc = jnp.dot(x_ref[...], wb_ref[...], preferred_element_type=F32)
    if n_resid:
        acc = alpha * _resid_value(r_refs) + acc
    if len(o_ref.shape) == 3:
        o_ref[0] = acc
        o_ref[1:] = jnp.zeros((o_ref.shape[0] - 1,) + acc.shape, o_ref.dtype)
    else:
        o_ref[...] = acc.astype(o_ref.dtype)

    if emit_bf16:
        @pl.when((j == pl.num_programs(0) - 1) & (i == pl.num_programs(1) - 1))
        def _last_wait():
            panel_copy(j).wait()


def _proj(x, w3, layer, *, n, tm, tn, resid=None, alpha=1.0, stack=None, emit_bf16=False, name):
    m, k = x.shape
    grid = (n // tn, m // tm)
    in_specs = [pl.BlockSpec((tm, k), lambda j, i: (i, 0)),
                pl.BlockSpec((None, k, tn), lambda j, i: (layer, 0, j))]
    args = [x, w3]
    n_resid = 0
    if resid is not None:
        r_specs, r_args = _resid_operands(resid, tm, tn, lambda j, i: (i, j))
        in_specs += r_specs
        args += r_args
        n_resid = len(r_args)
    aliases = {}
    if stack is None:
        out_shapes = [jax.ShapeDtypeStruct((m, n), F32)]
        out_specs = [pl.BlockSpec((tm, tn), lambda j, i: (i, j))]
    else:
        depth, prev = stack
        out_shapes = [jax.ShapeDtypeStruct((depth, m, n), F32)]
        out_specs = [pl.BlockSpec((None, tm, tn), lambda j, i: (layer, i, j))]
        if prev is None:
            assert layer == 0
            out_specs = [pl.BlockSpec((depth, tm, tn), lambda j, i: (0, i, j))]
        else:
            in_specs.append(pl.BlockSpec(memory_space=pl.ANY))
            aliases = {len(args): 0}
            args.append(prev)
    scratch = [pltpu.VMEM((k, tn), BF16)]
    if emit_bf16:
        out_shapes.append(jax.ShapeDtypeStruct((k, n), BF16))
        out_specs.append(pl.BlockSpec(memory_space=pl.ANY))
        scratch.append(pltpu.SemaphoreType.DMA(()))
    outs = pl.pallas_call(
        functools.partial(_proj_kernel, alpha=alpha, n_resid=n_resid, has_alias=bool(aliases),
                          emit_bf16=emit_bf16, tn=tn),
        grid=grid,
        in_specs=in_specs,
        out_specs=out_specs,
        out_shape=out_shapes,
        scratch_shapes=scratch,
        input_output_aliases=aliases,
        compiler_params=_params(2),
        name=name,
    )(*args)
    return tuple(outs) if emit_bf16 else outs[0]


def _proj_q(x, wq, *, tn, layer, stack, name):
    m, k = x.shape
    n = wq.shape[1]
    depth, prev = stack
    in_specs = [pl.BlockSpec((m, k), lambda j: (0, 0)), pl.BlockSpec((k, tn), lambda j: (0, j))]
    args = [x, wq]
    aliases = {}
    if prev is None:
        assert layer == 0
        out_spec = pl.BlockSpec((depth, m, tn), lambda j: (0, 0, j))
        kern = _proj_q_first_kernel
    else:
        out_spec = pl.BlockSpec((None, m, tn), lambda j: (layer, 0, j))
        in_specs.append(pl.BlockSpec(memory_space=pl.ANY))
        aliases = {2: 0}
        args.append(prev)
        kern = _proj_q_next_kernel
    return pl.pallas_call(
        kern,
        grid=(n // tn,),
        in_specs=in_specs,
        out_specs=out_spec,
        out_shape=jax.ShapeDtypeStruct((depth, m, n), F32),
        input_output_aliases=aliases,
        compiler_params=_params(1),
        name=name,
    )(*args)


def _proj_q_first_kernel(x_ref, w_ref, o_ref):
    o_ref[0] = jnp.dot(x_ref[...], w_ref[...], preferred_element_type=F32)
    o_ref[1:] = jnp.zeros((o_ref.shape[0] - 1,) + o_ref.shape[1:], o_ref.dtype)


def _proj_q_next_kernel(x_ref, w_ref, prev_ref, o_ref):
    del prev_ref
    o_ref[...] = jnp.dot(x_ref[...], w_ref[...], preferred_element_type=F32)


def _down_kernel(*refs, alpha, rem):
    x_ref, w_ref = refs[:2]
    r_refs = refs[2:-1]
    o_ref = refs[-1]
    kk = pl.program_id(2)
    nk = pl.num_programs(2)

    @pl.when(kk == 0)
    def _first():
        o_ref[...] = alpha * _resid_value(r_refs) + jnp.dot(x_ref[...], w_ref[...].astype(BF16),
                                                            preferred_element_type=F32)

    @pl.when((kk > 0) & (kk < nk - 1))
    def _full():
        o_ref[...] += jnp.dot(x_ref[...], w_ref[...].astype(BF16), preferred_element_type=F32)

    @pl.when(kk == nk - 1)
    def _last():
        o_ref[...] += jnp.dot(x_ref[:, :rem], w_ref[:rem, :].astype(BF16), preferred_element_type=F32)


def _down(x, w3, layer, resid, *, tm, tn, tk):
    m, k = x.shape
    n = w3.shape[2]
    nk = pl.cdiv(k, tk)
    assert nk >= 2
    rem = k - (nk - 1) * tk
    r_specs, r_args = _resid_operands(resid, tm, tn, lambda j, i, kk: (i, j))
    return pl.pallas_call(
        functools.partial(_down_kernel, alpha=ALPHA, rem=rem),
        grid=(n // tn, m // tm, nk),
        in_specs=[pl.BlockSpec((tm, tk), lambda j, i, kk: (i, kk)),
                  pl.BlockSpec((None, tk, tn), lambda j, i, kk: (layer, kk, j))] + r_specs,
        out_specs=pl.BlockSpec((tm, tn), lambda j, i, kk: (i, j)),
        out_shape=jax.ShapeDtypeStruct((m, n), F32),
        compiler_params=_params(3),
        name="down_proj",
    )(x, w3, *r_args)


def _ln_kernel(z_ref, g_ref, b_ref, *o_refs, full):
    z = z_ref[...]
    mu = jnp.mean(z, axis=-1, keepdims=True)
    zc = z - mu
    var = jnp.mean(zc * zc, axis=-1, keepdims=True)
    rstd = lax.rsqrt(var + LN_EPS)
    y = zc * rstd * g_ref[...] + b_ref[...]
    if full:
        o_refs[0][...] = y
    else:
        ob_ref, mu_ref, rstd_ref = o_refs
        ob_ref[...] = y.astype(BF16)
        mu_ref[...] = jnp.broadcast_to(mu, mu_ref.shape)
        rstd_ref[...] = jnp.broadcast_to(rstd, rstd_ref.shape)


def _layer_norm(z, g3, b3, layer, *, tm, full=False):
    m, d = z.shape
    row = pl.BlockSpec((tm, d), lambda i: (i, 0))
    col = pl.BlockSpec((tm, LANES), lambda i: (i, 0))
    if full:
        out_specs, out_shape = [row], [jax.ShapeDtypeStruct((m, d), F32)]
    else:
        out_specs = [row, col, col]
        out_shape = [jax.ShapeDtypeStruct((m, d), BF16), jax.ShapeDtypeStruct((m, LANES), F32),
                     jax.ShapeDtypeStruct((m, LANES), F32)]
    outs = pl.pallas_call(
        functools.partial(_ln_kernel, full=full),
        grid=(m // tm,),
        in_specs=[row,
                  pl.BlockSpec((None, 1, d), lambda i: (layer, 0, 0)),
                  pl.BlockSpec((None, 1, d), lambda i: (layer, 0, 0))],
        out_specs=out_specs,
        out_shape=out_shape,
        compiler_params=_params(1),
        name="layer_norm",
    )(z, g3, b3)
    return outs[0] if full else tuple(outs)


def _pool_groups(window_sum_fn, u_fn, inv_cnt_fn, wb_ref, sc_ref, y_ref):
    for g, w in enumerate(POOL_WINDOWS):
        cols = slice(g * POOL_GC, (g + 1) * POOL_GC)
        d = window_sum_fn(g, w, cols) * inv_cnt_fn(w) - u_fn(cols)
        y = jnp.dot(d.astype(BF16), wb_ref[g], preferred_element_type=F32) * sc_ref[:, cols]
        y_ref[:, cols] = y.astype(y_ref.dtype)


def _pool_seq_kernel(u_ref, halo_ref, wg_ref, sc_ref, y_ref, wb_ref, *, tm, tiles_per_seq):
    i = pl.program_id(0)

    @pl.when(i == 0)
    def _cast():
        wb_ref[...] = wg_ref[...].astype(BF16)

    t_in_seq = i % tiles_per_seq
    halo = jnp.where(t_in_seq == 0, 0.0, halo_ref[...])
    pos = t_in_seq * tm + lax.broadcasted_iota(jnp.int32, (tm, 1), 0)

    def window_sum(g, w, cols):
        s = jnp.concatenate([halo[:, cols], u_ref[:, cols]], axis=0)
        k = 1
        while k < w:
            s = s + pltpu.roll(s, k, 0)
            k *= 2
        return s[HALO:]

    def inv_cnt(w):
        return 1.0 / jnp.minimum(pos + 1, w).astype(F32)

    _pool_groups(window_sum, lambda cols: u_ref[:, cols], inv_cnt, wb_ref, sc_ref, y_ref)


def _pool_seq(h, w_pool_grp, pool_scale, layer, *, tm):
    m = h.shape[1]
    tiles_per_seq = SEQ // tm
    return pl.pallas_call(
        functools.partial(_pool_seq_kernel, tm=tm, tiles_per_seq=tiles_per_seq),
        grid=(m // tm,),
        in_specs=[pl.BlockSpec((None, tm, POOL_W), lambda i: (layer, i, 0)),
                  pl.BlockSpec((None, HALO, POOL_W),
                               lambda i: (layer, jnp.maximum(i * (tm // HALO) - 1, 0), 0)),
                  pl.BlockSpec((None, len(POOL_WINDOWS), POOL_GC, POOL_GC), lambda i: (layer, 0, 0, 0)),
                  pl.BlockSpec((None, 1, POOL_W), lambda i: (layer, 0, 0))],
        out_specs=pl.BlockSpec((tm, POOL_W), lambda i: (i, 0)),
        out_shape=jax.ShapeDtypeStruct((m, POOL_W), BF16),
        scratch_shapes=[pltpu.VMEM((len(POOL_WINDOWS), POOL_GC, POOL_GC), BF16)],
        compiler_params=_params(1),
        name="pool_seq",
    )(h, h, w_pool_grp, pool_scale)


def _pool_dec_kernel(u_ref, st_ref, wg_ref, sc_ref, y_ref, wb_ref):
    wb_ref[...] = wg_ref[...].astype(BF16)

    def window_sum(g, w, cols):
        s = u_ref[:, cols]
        for k in range(1, w):
            s = s + st_ref[POOL_CTX - k, :, cols]
        return s

    def inv_cnt(w):
        return 1.0 / float(min(PAST_LEN + 1, w))

    _pool_groups(window_sum, lambda cols: u_ref[:, cols], inv_cnt, wb_ref, sc_ref, y_ref)


def _pool_dec(h, st, w_pool_grp, pool_scale, layer):
    m = h.shape[1]
    return pl.pallas_call(
        _pool_dec_kernel,
        grid=(1,),
        in_specs=[pl.BlockSpec((None, m, POOL_W), lambda i: (layer, 0, 0)),
                  pl.BlockSpec((None, POOL_CTX, m, POOL_W), lambda i: (layer, 0, 0, 0)),
                  pl.BlockSpec((None, len(POOL_WINDOWS), POOL_GC, POOL_GC), lambda i: (layer, 0, 0, 0)),
                  pl.BlockSpec((None, 1, POOL_W), lambda i: (layer, 0, 0))],
        out_specs=pl.BlockSpec((m, POOL_W), lambda i: (0, 0)),
        out_shape=jax.ShapeDtypeStruct((m, POOL_W), BF16),
        scratch_shapes=[pltpu.VMEM((len(POOL_WINDOWS), POOL_GC, POOL_GC), BF16)],
        compiler_params=_params(1),
        name="pool_dec",
    )(h, st, w_pool_grp, pool_scale)


def _merge_groups(o_list, lse_list):
    mx = functools.reduce(jnp.maximum, lse_list)
    ws = [jnp.exp(l - mx) for l in lse_list]
    num = functools.reduce(lambda a, b: a + b, [w * o for w, o in zip(ws, o_list)])
    den = functools.reduce(lambda a, b: a + b, ws)
    return num / den


def _attn_seq_kernel(*refs):
    qkv = refs[:9]
    o_ref = refs[9]
    og_refs = refs[10:13]
    lg_refs = refs[13:16]

    qi = lax.broadcasted_iota(jnp.int32, (ATT_BLK, ATT_BLK), 0)
    kj = lax.broadcasted_iota(jnp.int32, (ATT_BLK, ATT_BLK), 1)
    cur_mask = kj <= qi
    qi2 = lax.broadcasted_iota(jnp.int32, (ATT_BLK, 2 * ATT_BLK), 0)
    kj2 = lax.broadcasted_iota(jnp.int32, (ATT_BLK, 2 * ATT_BLK), 1)
    both_mask = (kj2 >= qi2) & (kj2 - qi2 <= ATT_BLK)

    for g, (win, dil) in enumerate(ATT_GROUPS):
        q_ref, k_ref, v_ref = qkv[3 * g:3 * g + 3]
        n_blk = SEQ // dil // ATT_BLK
        for r in range(dil):
            for nb in range(n_blk):
                q_rows = pl.ds(r + nb * ATT_BLK * dil, ATT_BLK, stride=dil)
                if nb == 0:
                    kv_rows, mask = q_rows, cur_mask
                else:
                    kv_rows = pl.ds(r + (nb - 1) * ATT_BLK * dil, 2 * ATT_BLK, stride=dil)
                    mask = both_mask
                q = q_ref[q_rows, :].astype(BF16)
                k = k_ref[kv_rows, :].astype(BF16)
                v = v_ref[kv_rows, :].astype(BF16)
                s = lax.dot_general(q, k, (((1,), (1,)), ((), ())), preferred_element_type=F32) * SCALE
                s = jnp.where(mask, s, NEG)
                mx = jnp.max(s, axis=-1, keepdims=True)
                p = jnp.exp(s - mx)
                den = jnp.sum(p, axis=-1, keepdims=True)
                o = jnp.dot(p.astype(BF16), v, preferred_element_type=F32) / den
                og_refs[g][q_rows, :] = o
                lg_refs[g][q_rows, :] = jnp.broadcast_to(mx + jnp.log(den), (ATT_BLK, HEAD_DIM))

    chunk = 256
    for c in range(SEQ // chunk):
        rows = pl.ds(c * chunk, chunk)
        out = _merge_groups([r_[rows, :] for r_ in og_refs], [r_[rows, :] for r_ in lg_refs])
        o_ref[rows, :] = out.astype(o_ref.dtype)


def _attn_seq(h, layer):
    m = h.shape[1]
    in_specs = []
    for g in range(N_GROUPS):
        for which in range(3):
            blk0 = (POOL_W + (3 * g + which) * ATT_W) // HEAD_DIM
            in_specs.append(pl.BlockSpec((None, SEQ, HEAD_DIM),
                                         lambda b, hd, blk0=blk0: (layer, b, blk0 + hd)))
    return pl.pallas_call(
        _attn_seq_kernel,
        grid=(m // SEQ, N_HEADS),
        in_specs=in_specs,
        out_specs=pl.BlockSpec((SEQ, HEAD_DIM), lambda b, hd: (b, hd)),
        out_shape=jax.ShapeDtypeStruct((m, ATT_W), BF16),
        scratch_shapes=[pltpu.VMEM((SEQ, HEAD_DIM), F32)] * (2 * N_GROUPS),
        compiler_params=_params(2),
        name="attn_seq",
    )(*([h] * 9))


def _attn_dec_kernel(h_ref, c1_ref, c2_ref, c3_ref, o_ref):
    def rnd(t):
        return t.astype(BF16).astype(F32)

    outs, lses = [], []
    for g, c_ref in enumerate((c1_ref, c2_ref, c3_ref)):
        r0 = (POOL_W + 3 * g * ATT_W) // HEAD_DIM
        q = rnd(h_ref[r0:r0 + N_HEADS, :])
        k_new = rnd(h_ref[r0 + N_HEADS:r0 + 2 * N_HEADS, :])
        v_new = rnd(h_ref[r0 + 2 * N_HEADS:r0 + 3 * N_HEADS, :])
        kc = rnd(c_ref[:, 0])
        vc = rnd(c_ref[:, 1])
        s = jnp.sum(kc * q[None], axis=-1, keepdims=True) * SCALE
        s_new = jnp.sum(k_new * q, axis=-1, keepdims=True) * SCALE
        mx = jnp.maximum(jnp.max(s, axis=0), s_new)
        p = jnp.exp(s - mx[None])
        p_new = jnp.exp(s_new - mx)
        den = jnp.sum(p, axis=0) + p_new
        acc = jnp.sum(rnd(p) * vc, axis=0) + rnd(p_new) * v_new
        outs.append(acc / den)
        lses.append(mx + jnp.log(den))
    o_ref[...] = _merge_groups(outs, lses)


def _attn_dec(h4, caches, layer):
    in_specs = [pl.BlockSpec((None, None, IN_W // HEAD_DIM, HEAD_DIM), lambda b: (layer, b, 0, 0))]
    for c in caches:
        in_specs.append(pl.BlockSpec((None, None, c.shape[2], None, 2, N_HEADS, HEAD_DIM),
                                     lambda b: (layer, b, 0, 0, 0, 0, 0)))
    return pl.pallas_call(
        _attn_dec_kernel,
        grid=(DEC_BATCH,),
        in_specs=in_specs,
        out_specs=pl.BlockSpec((None, N_HEADS, HEAD_DIM), lambda b: (b, 0, 0)),
        out_shape=jax.ShapeDtypeStruct((DEC_BATCH, N_HEADS, HEAD_DIM), F32),
        compiler_params=_params(1),
        name="attn_dec",
    )(h4, *caches)


def _gate_kernel(yp_ref, oa_ref, wp_ref, wa_ref, gp_ref, ga_ref, o_ref, wpb_ref, wab_ref):
    @pl.when(pl.program_id(1) == 0)
    def _cast():
        wpb_ref[...] = wp_ref[...].astype(BF16)
        wab_ref[...] = wa_ref[...].astype(BF16)

    bp = jnp.dot(yp_ref[...], wpb_ref[...], preferred_element_type=F32)
    ba = jnp.dot(oa_ref[...], wab_ref[...], preferred_element_type=F32)
    merged = jax.nn.sigmoid(gp_ref[...]) * bp + jax.nn.sigmoid(ga_ref[...]) * ba
    o_ref[...] = merged.astype(o_ref.dtype)


def _gate_merge(y_pool, o_att, h, w_br_pool, w_br_att, layer, *, tm, tn):
    m = h.shape[1]
    gp0 = GATE_OFF // tn
    ga0 = (GATE_OFF + D_MODEL) // tn
    return pl.pallas_call(
        _gate_kernel,
        grid=(D_MODEL // tn, m // tm),
        in_specs=[pl.BlockSpec((tm, POOL_W), lambda j, i: (i, 0)),
                  pl.BlockSpec((tm, ATT_W), lambda j, i: (i, 0)),
                  pl.BlockSpec((None, POOL_W, tn), lambda j, i: (layer, 0, j)),
                  pl.BlockSpec((None, ATT_W, tn), lambda j, i: (layer, 0, j)),
                  pl.BlockSpec((None, tm, tn), lambda j, i: (layer, i, gp0 + j)),
                  pl.BlockSpec((None, tm, tn), lambda j, i: (layer, i, ga0 + j))],
        out_specs=pl.BlockSpec((tm, tn), lambda j, i: (i, j)),
        out_shape=jax.ShapeDtypeStruct((m, D_MODEL), BF16),
        scratch_shapes=[pltpu.VMEM((POOL_W, tn), BF16), pltpu.VMEM((ATT_W, tn), BF16)],
        compiler_params=_params(2),
        name="gate_merge",
    )(y_pool, o_att, w_br_pool, w_br_att, h, h)


def _gelu_gate(a2, a1, a, b, cw_ref, cb_ref):
    ac = cw_ref[0:1] * a2 + cw_ref[1:2] * a1 + cw_ref[2:3] * a + cb_ref[...]
    gelu = 0.5 * ac * (1.0 + lax.erf(ac * (0.5 ** 0.5)))
    return gelu * b


def _ffn_up_seq_kernel(x_ref, wa_ref, wb_ref, cw_ref, cb_ref, hm_ref, tail_ref, wq_ref,
                       wab_ref, wbb_ref, carry_ref, sem, *, tm, tn, tiles_per_seq, n_chunks):
    j, i = pl.program_id(0), pl.program_id(1)
    n_panels = pl.num_programs(0)

    def panel_copies(jj):
        return (pltpu.make_async_copy(wab_ref, wq_ref.at[:, pl.ds(jj * tn, tn)], sem.at[0]),
                pltpu.make_async_copy(wbb_ref, wq_ref.at[:, pl.ds((n_panels + jj) * tn, tn)], sem.at[1]))

    @pl.when(i == 0)
    def _cast():
        @pl.when(j > 0)
        def _drain():
            for c in panel_copies(j - 1):
                c.wait()
        wab_ref[...] = wa_ref[...].astype(BF16)
        wbb_ref[...] = wb_ref[...].astype(BF16)
        for c in panel_copies(j):
            c.start()

    @pl.when(i % tiles_per_seq == 0)
    def _reset():
        carry_ref[...] = jnp.zeros_like(carry_ref)

    tc = tm // n_chunks
    row = lax.broadcasted_iota(jnp.int32, (tc, 1), 0)
    carry = carry_ref[...]
    for c in range(n_chunks):
        rows = pl.ds(c * tc, tc)
        x = x_ref[rows, :]
        a = jnp.dot(x, wab_ref[...], preferred_element_type=F32)
        b = jnp.dot(x, wbb_ref[...], preferred_element_type=F32)
        a1 = jnp.where(row == 0, carry[7:8], pltpu.roll(a, 1, 0))
        a2 = jnp.where(row == 0, carry[6:7], jnp.where(row == 1, carry[7:8], pltpu.roll(a, 2, 0)))
        hm_ref[rows, :] = _gelu_gate(a2, a1, a, b, cw_ref, cb_ref).astype(hm_ref.dtype)
        carry = a[tc - 8:]
    carry_ref[...] = carry
    tail_ref[0] = carry

    @pl.when((j == n_panels - 1) & (i == pl.num_programs(1) - 1))
    def _last_wait():
        for c in panel_copies(j):
            c.wait()


def _ffn_up_seq(x, w_up, conv_w, conv_b, layer, *, tm, tn, n_chunks):
    m, k = x.shape
    n_panels = D_FF // tn
    return pl.pallas_call(
        functools.partial(_ffn_up_seq_kernel, tm=tm, tn=tn, tiles_per_seq=SEQ // tm, n_chunks=n_chunks),
        grid=(n_panels, m // tm),
        in_specs=[pl.BlockSpec((tm, k), lambda j, i: (i, 0)),
                  pl.BlockSpec((None, k, tn), lambda j, i: (layer, 0, j)),
                  pl.BlockSpec((None, k, tn), lambda j, i: (layer, 0, n_panels + j)),
                  pl.BlockSpec((None, 3, tn), lambda j, i: (layer, 0, j)),
                  pl.BlockSpec((None, 1, tn), lambda j, i: (layer, 0, j))],
        out_specs=[pl.BlockSpec((tm, tn), lambda j, i: (i, j)),
                   pl.BlockSpec((1, 8, tn), lambda j, i: (i, 0, j)),
                   pl.BlockSpec(memory_space=pl.ANY)],
        out_shape=[jax.ShapeDtypeStruct((m, D_FF), BF16),
                   jax.ShapeDtypeStruct((m // tm, 8, D_FF), F32),
                   jax.ShapeDtypeStruct((k, 2 * D_FF), BF16)],
        scratch_shapes=[pltpu.VMEM((k, tn), BF16), pltpu.VMEM((k, tn), BF16),
                        pltpu.VMEM((8, tn), F32), pltpu.SemaphoreType.DMA((2,))],
        compiler_params=_params(2),
        name="ffn_up",
    )(x, w_up, w_up, conv_w, conv_b)


def _ffn_up_dec_kernel(x_ref, wa_ref, wb_ref, cw_ref, cb_ref, p2_ref, p1_ref, hm_ref, a_ref):
    x = x_ref[...]
    a = jnp.dot(x, wa_ref[...], preferred_element_type=F32)
    b = jnp.dot(x, wb_ref[...], preferred_element_type=F32)
    a_ref[...] = a
    hm_ref[...] = _gelu_gate(p2_ref[...], p1_ref[...], a, b, cw_ref, cb_ref).astype(hm_ref.dtype)


def _ffn_up_dec(x, wq_up, conv_w, conv_b, layer, prev, *, tn):
    m, k = x.shape
    n_panels = D_FF // tn
    return pl.pallas_call(
        _ffn_up_dec_kernel,
        grid=(n_panels,),
        in_specs=[pl.BlockSpec((m, k), lambda j: (0, 0)),
                  pl.BlockSpec((k, tn), lambda j: (0, j)),
                  pl.BlockSpec((k, tn), lambda j: (0, n_panels + j)),
                  pl.BlockSpec((None, 3, tn), lambda j: (layer, 0, j)),
                  pl.BlockSpec((None, 1, tn), lambda j: (layer, 0, j)),
                  pl.BlockSpec((None, m, tn), lambda j: (0, 0, j)),
                  pl.BlockSpec((None, m, tn), lambda j: (1, 0, j))],
        out_specs=[pl.BlockSpec((m, tn), lambda j: (0, j)), pl.BlockSpec((m, tn), lambda j: (0, j))],
        out_shape=[jax.ShapeDtypeStruct((m, D_FF), BF16), jax.ShapeDtypeStruct((m, D_FF), F32)],
        compiler_params=_params(1),
        name="ffn_up_dec",
    )(x, wq_up, wq_up, conv_w, conv_b, prev, prev)


def _layer(resid, xb, h_prev, layer, p, *, dec=None):
    m = xb.shape[0]
    t = _tiles(m)
    out = {}
    if dec is None:
        h, out['wq_in'] = _proj(xb, p['w_in'], layer, n=IN_W, tm=t['proj'][0], tn=t['proj'][1],
                                stack=(DEPTH, h_prev), emit_bf16=True, name="in_proj")
    else:
        h = _proj_q(xb, dec['wq_in'][layer], tn=t['proj'][1], layer=layer, stack=(DEPTH, h_prev),
                    name="in_proj_dec")
    if dec is None:
        y_pool = _pool_seq(h, p['w_pool_grp'], p['pool_scale'], layer, tm=t['pool'])
        o_att = _attn_seq(h, layer)
    else:
        y_pool = _pool_dec(h, dec['pool'], p['w_pool_grp'], p['pool_scale'], layer)
        h4 = h.reshape(DEPTH, m, IN_W // HEAD_DIM, HEAD_DIM)
        o = _attn_dec(h4, dec['caches'], layer).reshape(DEC_BATCH, ATT_W)
        o_att = jnp.pad(o, ((0, m - DEC_BATCH), (0, 0))).astype(BF16)
    merged = _gate_merge(y_pool, o_att, h, p['w_br_pool'], p['w_br_att'], layer,
                         tm=t['gate'][0], tn=t['gate'][1])
    z = _proj(merged, p['w_out'], layer, n=D_MODEL, tm=t['proj'][0], tn=t['proj'][1],
              resid=resid, alpha=ALPHA, name="out_proj")
    x1b, mu1, rstd1 = _layer_norm(z, p['ln1_g'], p['ln1_b'], layer, tm=t['ln'])
    if dec is None:
        hm, tail, out['wq_up'] = _ffn_up_seq(x1b, p['w_up'], p['conv_w'], p['conv_b'], layer,
                                             tm=t['ffn_up'][0], tn=t['ffn_up'][1],
                                             n_chunks=t['ffn_chunks'])
    else:
        hm, tail = _ffn_up_dec(x1b, dec['wq_up'][layer], p['conv_w'], p['conv_b'], layer,
                               dec['conv'][layer], tn=t['ffn_up'][1])
    z2 = _down(hm, p['w_down'], layer, (z, mu1, rstd1, p['ln1_g'], p['ln1_b'], layer),
               tm=t['down'][0], tn=t['down'][1], tk=t['down'][2])
    if layer == DEPTH - 1:
        out['y'] = _layer_norm(z2, p['ln2_g'], p['ln2_b'], layer, tm=t['ln'], full=True)
    else:
        out['xb'], mu2, rstd2 = _layer_norm(z2, p['ln2_g'], p['ln2_b'], layer, tm=t['ln'])
        out['resid'] = (z2, mu2, rstd2, p['ln2_g'], p['ln2_b'], layer)
    out.update(h=h, tail=tail)
    return out


def _run_prompt(x_prompt, p):
    batch = x_prompt.shape[0]
    resid = x_prompt.reshape(batch * SEQ, D_MODEL)
    xb = resid.astype(BF16)
    h, tails, wq_in, wq_up = None, [], [], []
    for l in range(DEPTH):
        o = _layer(resid, xb, h, l, p)
        h, resid, xb = o['h'], o.get('resid'), o.get('xb')
        tails.append(o['tail'])
        wq_in.append(o['wq_in'])
        wq_up.append(o['wq_up'])
    x = o['y']
    h4 = h.reshape(DEPTH, batch, SEQ, IN_W)
    pool = h4[:, :, SEQ - POOL_CTX:, :POOL_W]
    kvs = []
    for g, (win, _) in enumerate(ATT_GROUPS):
        base = POOL_W + 3 * g * ATT_W + ATT_W
        keep = min(win, SEQ)
        kvs.append(h4[:, :, SEQ - keep:, base:base + 2 * ATT_W]
                   .reshape(DEPTH, batch, keep, 2, N_HEADS, HEAD_DIM))
    tm = _tiles(batch * SEQ)['ffn_up'][0]
    t5 = jnp.stack(tails).reshape(DEPTH, batch, SEQ // tm, 8, D_FF)
    conv = t5[:, :, SEQ // tm - 1, 6:8, :]
    return (x.reshape(batch, SEQ, D_MODEL), pool, *kvs, conv), wq_in, wq_up


def _run_sample(x_sample, state_pool, caches_in, state_conv, p, wq_in, wq_up):
    pad = DEC_ROWS - DEC_BATCH
    resid = jnp.pad(x_sample.reshape(DEC_BATCH, D_MODEL), ((0, pad), (0, 0)))
    dec = dict(
        wq_in=wq_in, wq_up=wq_up,
        pool=jnp.pad(jnp.transpose(state_pool, (0, 2, 1, 3)), ((0, 0), (0, 0), (0, pad), (0, 0))),
        conv=jnp.pad(jnp.transpose(state_conv, (0, 2, 1, 3)), ((0, 0), (0, 0), (0, pad), (0, 0))),
        caches=[c.reshape(DEPTH, DEC_BATCH, c.shape[2] // dil, dil, 2, N_HEADS, HEAD_DIM)
                for c, (_, dil) in zip(caches_in, ATT_GROUPS)],
    )
    xb = resid.astype(BF16)
    h, a_new = None, []
    for l in range(DEPTH):
        o = _layer(resid, xb, h, l, p, dec=dec)
        h, resid, xb = o['h'], o.get('resid'), o.get('xb')
        a_new.append(o['tail'][:DEC_BATCH])
    x = o['y']
    h = h[:, :DEC_BATCH]
    pool = jnp.concatenate([state_pool[:, :, 1:], h[:, :, None, :POOL_W]], axis=2)
    kvs = []
    for g, (win, _) in enumerate(ATT_GROUPS):
        base = POOL_W + 3 * g * ATT_W + ATT_W
        kv_new = h[:, :, base:base + 2 * ATT_W].reshape(DEPTH, DEC_BATCH, 1, 2, N_HEADS, HEAD_DIM)
        keep = min(win, PAST_LEN + 1)
        kvs.append(jnp.concatenate([caches_in[g], kv_new], axis=2)[:, :, caches_in[g].shape[2] + 1 - keep:])
    conv = jnp.concatenate([state_conv[:, :, 1:], jnp.stack(a_new)[:, :, None, :]], axis=2)
    return (x[:DEC_BATCH].reshape(DEC_BATCH, 1, D_MODEL), pool, *kvs, conv)


def kernel(x_prompt, x_sample, state_pool, cache_kv1, cache_kv2, cache_kv3, state_conv, w_in, w_pool_grp, pool_scale, w_br_pool, w_br_att, w_out, ln1_g, ln1_b, w_up, conv_w, conv_b, w_down, ln2_g, ln2_b):
    def row3(t):
        return t.reshape(DEPTH, 1, t.shape[-1])

    p = dict(w_in=w_in, w_pool_grp=w_pool_grp, pool_scale=row3(pool_scale), w_br_pool=w_br_pool,
             w_br_att=w_br_att, w_out=w_out, ln1_g=row3(ln1_g), ln1_b=row3(ln1_b), w_up=w_up,
             conv_w=conv_w, conv_b=row3(conv_b), w_down=w_down, ln2_g=row3(ln2_g), ln2_b=row3(ln2_b))
    (yp, pool_p, kv1_p, kv2_p, kv3_p, conv_p), wq_in, wq_up = _run_prompt(x_prompt, p)
    ys, pool_s, kv1_s, kv2_s, kv3_s, conv_s = _run_sample(
        x_sample, state_pool, (cache_kv1, cache_kv2, cache_kv3), state_conv, p, wq_in, wq_up)
    return (yp, ys, pool_p, pool_s, kv1_p, kv1_s, kv2_p, kv2_s, kv3_p, kv3_s, conv_p, conv_s)
```

```python
import functools

import jax
import jax.numpy as jnp
from jax import lax
from jax.experimental import pallas as pl
from jax.experimental.pallas import tpu as pltpu

D_MODEL = 4096
SEQ = 2048
DEPTH = 2
DEC_BATCH = 8
PAST_LEN = 16384
HEAD_DIM = 128
N_HEADS = 8
ATT_GROUPS = ((128, 1), (512, 4), (2048, 16))
N_GROUPS = len(ATT_GROUPS)
ATT_W = N_HEADS * HEAD_DIM
POOL_WINDOWS = (2, 4, 8, 16)
POOL_W = D_MODEL // 2
POOL_GC = POOL_W // len(POOL_WINDOWS)
POOL_CTX = max(POOL_WINDOWS) - 1
D_FF = 11008
GATE_OFF = POOL_W + 3 * N_GROUPS * ATT_W
IN_W = GATE_OFF + 2 * D_MODEL
ALPHA = (2.0 * DEPTH) ** 0.25
LN_EPS = 1e-5
NEG = -1e30
ATT_BLK = 128
SCALE = HEAD_DIM ** -0.5

LANES = 128
DEC_ROWS = 16
HALO = 16
VMEM_LIMIT = 62 * 1024 * 1024

F32 = jnp.float32
BF16 = jnp.bfloat16

assert all(PAST_LEN >= win and win % dil == 0 for win, dil in ATT_GROUPS)


def _tiles(m):
    big = m >= 1024
    return dict(
        proj=(1024 if big else m, 1024),
        proj_first=(512 if big else m, 1024),
        gate=(512 if big else m, 1024),
        ffn_up=(2048 if big else m, 256),
        ffn_chunks=4,
        down=(2048, 1024, 1024) if big else (m, 1024, 2048),
        ln=512 if big else m,
        pool=512,
    )


def _params(n_axes, flags=None):
    return pltpu.CompilerParams(dimension_semantics=("arbitrary",) * n_axes,
                                vmem_limit_bytes=VMEM_LIMIT, flags=flags)


def _resid_operands(resid, tm, tn, at):
    if not isinstance(resid, tuple):
        return [pl.BlockSpec((tm, tn), lambda *g: at(*g))], [resid]
    z, mu, rstd, g3, b3, ln_layer = resid
    specs = [pl.BlockSpec((tm, tn), lambda *g: at(*g)),
             pl.BlockSpec((tm, LANES), lambda *g: (at(*g)[0], 0)),
             pl.BlockSpec((tm, LANES), lambda *g: (at(*g)[0], 0)),
             pl.BlockSpec((None, 1, tn), lambda *g: (ln_layer, 0, at(*g)[1])),
             pl.BlockSpec((None, 1, tn), lambda *g: (ln_layer, 0, at(*g)[1]))]
    return specs, [z, mu, rstd, g3, b3]


def _resid_value(r_refs):
    if len(r_refs) == 1:
        return r_refs[0][...]
    z_ref, mu_ref, rstd_ref, g_ref, b_ref = r_refs
    mu, rstd = mu_ref[...], rstd_ref[...]
    parts = []
    for c in range(z_ref.shape[1] // LANES):
        cols = slice(c * LANES, (c + 1) * LANES)
        parts.append((z_ref[:, cols] - mu) * rstd * g_ref[:, cols] + b_ref[:, cols])
    return jnp.concatenate(parts, axis=1)


def _proj_kernel(*refs, alpha, n_resid, has_alias, emit_bf16, stream_w, layer, tn):
    refs = list(refs)
    x_ref, w_ref = refs[:2]
    r_refs = refs[2:2 + n_resid]
    n_in = 2 + n_resid + int(has_alias)
    o_ref = refs[n_in]
    wq_ref = refs[n_in + 1] if emit_bf16 else None
    scratch = refs[n_in + 1 + int(emit_bf16):]
    wb_ref = scratch[0]
    wf_ref, wf_sem = (scratch[1], scratch[2]) if stream_w else (None, None)
    sem = refs[-1] if emit_bf16 else None
    j, i = pl.program_id(0), pl.program_id(1)
    n_j = pl.num_programs(0)

    def panel_copy(jj):
        return pltpu.make_async_copy(wb_ref, wq_ref.at[:, pl.ds(jj * tn, tn)], sem)

    def panel_fetch(jj):
        return pltpu.make_async_copy(w_ref.at[layer, :, pl.ds(jj * tn, tn)], wf_ref, wf_sem)

    if stream_w:
        @pl.when((j == 0) & (i == 0))
        def _first_fetch():
            panel_fetch(0).start()

    @pl.when(i == 0)
    def _cast():
        if emit_bf16:
            @pl.when(j > 0)
            def _drain():
                panel_copy(j - 1).wait()
        if stream_w:
            panel_fetch(j).wait()
            wb_ref[...] = wf_ref[...].astype(BF16)
        else:
            wb_ref[...] = w_ref[...].astype(BF16)
        if emit_bf16:
            panel_copy(j).start()

    if stream_w:
        @pl.when((i == 1) & (j + 1 < n_j))
        def _next_fetch():
            panel_fetch(j + 1).start()

    acc = jnp.dot(x_ref[...], wb_ref[...], preferred_element_type=F32)
    if n_resid:
        acc = alpha * _resid_value(r_refs) + acc
    if len(o_ref.shape) == 3:
        o_ref[0] = acc
        o_ref[1:] = jnp.zeros((o_ref.shape[0] - 1,) + acc.shape, o_ref.dtype)
    else:
        o_ref[...] = acc.astype(o_ref.dtype)

    if emit_bf16:
        @pl.when((j == pl.num_programs(0) - 1) & (i == pl.num_programs(1) - 1))
        def _last_wait():
            panel_copy(j).wait()


def _proj(x, w3, layer, *, n, tm, tn, resid=None, alpha=1.0, stack=None, emit_bf16=False, name):
    m, k = x.shape
    grid = (n // tn, m // tm)
    stream_w = grid[1] > 1
    w_spec = (pl.BlockSpec(memory_space=pl.ANY) if stream_w
              else pl.BlockSpec((None, k, tn), lambda j, i: (layer, 0, j)))
    in_specs = [pl.BlockSpec((tm, k), lambda j, i: (i, 0)), w_spec]
    args = [x, w3]
    n_resid = 0
    if resid is not None:
        r_specs, r_args = _resid_operands(resid, tm, tn, lambda j, i: (i, j))
        in_specs += r_specs
        args += r_args
        n_resid = len(r_args)
    aliases = {}
    if stack is None:
        out_shapes = [jax.ShapeDtypeStruct((m, n), F32)]
        out_specs = [pl.BlockSpec((tm, tn), lambda j, i: (i, j))]
    else:
        depth, prev = stack
        out_shapes = [jax.ShapeDtypeStruct((depth, m, n), F32)]
        out_specs = [pl.BlockSpec((None, tm, tn), lambda j, i: (layer, i, j))]
        if prev is None:
            assert layer == 0
            out_specs = [pl.BlockSpec((depth, tm, tn), lambda j, i: (0, i, j))]
        else:
            in_specs.append(pl.BlockSpec(memory_space=pl.ANY))
            aliases = {len(args): 0}
            args.append(prev)
    scratch = [pltpu.VMEM((k, tn), BF16)]
    if stream_w:
        scratch += [pltpu.VMEM((k, tn), F32), pltpu.SemaphoreType.DMA(())]
    if emit_bf16:
        out_shapes.append(jax.ShapeDtypeStruct((k, n), BF16))
        out_specs.append(pl.BlockSpec(memory_space=pl.ANY))
        scratch.append(pltpu.SemaphoreType.DMA(()))
    outs = pl.pallas_call(
        functools.partial(_proj_kernel, alpha=alpha, n_resid=n_resid, has_alias=bool(aliases),
                          emit_bf16=emit_bf16, stream_w=stream_w, layer=layer, tn=tn),
        grid=grid,
        in_specs=in_specs,
        out_specs=out_specs,
        out_shape=out_shapes,
        scratch_shapes=scratch,
        input_output_aliases=aliases,
        compiler_params=_params(2),
        name=name,
    )(*args)
    return tuple(outs) if emit_bf16 else outs[0]


def _proj_q(x, wq, *, tn, layer, stack, name):
    m, k = x.shape
    n = wq.shape[1]
    depth, prev = stack
    in_specs = [pl.BlockSpec((m, k), lambda j: (0, 0)), pl.BlockSpec((k, tn), lambda j: (0, j))]
    args = [x, wq]
    aliases = {}
    if prev is None:
        assert layer == 0
        out_spec = pl.BlockSpec((depth, m, tn), lambda j: (0, 0, j))
        kern = _proj_q_first_kernel
    else:
        out_spec = pl.BlockSpec((None, m, tn), lambda j: (layer, 0, j))
        in_specs.append(pl.BlockSpec(memory_space=pl.ANY))
        aliases = {2: 0}
        args.append(prev)
        kern = _proj_q_next_kernel
    return pl.pallas_call(
        kern,
        grid=(n // tn,),
        in_specs=in_specs,
        out_specs=out_spec,
        out_shape=jax.ShapeDtypeStruct((depth, m, n), F32),
        input_output_aliases=aliases,
        compiler_params=_params(1),
        name=name,
    )(*args)


def _proj_q_first_kernel(x_ref, w_ref, o_ref):
    o_ref[0] = jnp.dot(x_ref[...], w_ref[...], preferred_element_type=F32)
    o_ref[1:] = jnp.zeros((o_ref.shape[0] - 1,) + o_ref.shape[1:], o_ref.dtype)


def _proj_q_next_kernel(x_ref, w_ref, prev_ref, o_ref):
    del prev_ref
    o_ref[...] = jnp.dot(x_ref[...], w_ref[...], preferred_element_type=F32)


def _down_kernel(*refs, alpha, rem):
    x_ref, w_ref = refs[:2]
    r_refs = refs[2:-1]
    o_ref = refs[-1]
    kk = pl.program_id(2)
    nk = pl.num_programs(2)

    @pl.when(kk == 0)
    def _first():
        o_ref[...] = alpha * _resid_value(r_refs) + jnp.dot(x_ref[...], w_ref[...].astype(BF16),
                                                            preferred_element_type=F32)

    @pl.when((kk > 0) & (kk < nk - 1))
    def _full():
        o_ref[...] += jnp.dot(x_ref[...], w_ref[...].astype(BF16), preferred_element_type=F32)

    @pl.when(kk == nk - 1)
    def _last():
        o_ref[...] += jnp.dot(x_ref[:, :rem], w_ref[:rem, :].astype(BF16), preferred_element_type=F32)


def _down(x, w3, layer, resid, *, tm, tn, tk):
    m, k = x.shape
    n = w3.shape[2]
    nk = pl.cdiv(k, tk)
    assert nk >= 2
    rem = k - (nk - 1) * tk
    r_specs, r_args = _resid_operands(resid, tm, tn, lambda j, i, kk: (i, j))
    return pl.pallas_call(
        functools.partial(_down_kernel, alpha=ALPHA, rem=rem),
        grid=(n // tn, m // tm, nk),
        in_specs=[pl.BlockSpec((tm, tk), lambda j, i, kk: (i, kk)),
                  pl.BlockSpec((None, tk, tn), lambda j, i, kk: (layer, kk, j))] + r_specs,
        out_specs=pl.BlockSpec((tm, tn), lambda j, i, kk: (i, j)),
        out_shape=jax.ShapeDtypeStruct((m, n), F32),
        compiler_params=_params(3),
        name="down_proj",
    )(x, w3, *r_args)


def _ln_kernel(z_ref, g_ref, b_ref, *o_refs, full):
    z = z_ref[...]
    mu = jnp.mean(z, axis=-1, keepdims=True)
    zc = z - mu
    var = jnp.mean(zc * zc, axis=-1, keepdims=True)
    rstd = lax.rsqrt(var + LN_EPS)
    y = zc * rstd * g_ref[...] + b_ref[...]
    if full:
        o_refs[0][...] = y
    else:
        ob_ref, mu_ref, rstd_ref = o_refs
        ob_ref[...] = y.astype(BF16)
        mu_ref[...] = jnp.broadcast_to(mu, mu_ref.shape)
        rstd_ref[...] = jnp.broadcast_to(rstd, rstd_ref.shape)


def _layer_norm(z, g3, b3, layer, *, tm, full=False):
    m, d = z.shape
    row = pl.BlockSpec((tm, d), lambda i: (i, 0))
    col = pl.BlockSpec((tm, LANES), lambda i: (i, 0))
    if full:
        out_specs, out_shape = [row], [jax.ShapeDtypeStruct((m, d), F32)]
    else:
        out_specs = [row, col, col]
        out_shape = [jax.ShapeDtypeStruct((m, d), BF16), jax.ShapeDtypeStruct((m, LANES), F32),
                     jax.ShapeDtypeStruct((m, LANES), F32)]
    outs = pl.pallas_call(
        functools.partial(_ln_kernel, full=full),
        grid=(m // tm,),
        in_specs=[row,
                  pl.BlockSpec((None, 1, d), lambda i: (layer, 0, 0)),
                  pl.BlockSpec((None, 1, d), lambda i: (layer, 0, 0))],
        out_specs=out_specs,
        out_shape=out_shape,
        compiler_params=_params(1),
        name="layer_norm",
    )(z, g3, b3)
    return outs[0] if full else tuple(outs)


def _pool_groups(window_sum_fn, u_fn, inv_cnt_fn, wb_ref, sc_ref, y_ref):
    for g, w in enumerate(POOL_WINDOWS):
        cols = slice(g * POOL_GC, (g + 1) * POOL_GC)
        d = window_sum_fn(g, w, cols) * inv_cnt_fn(w) - u_fn(cols)
        y = jnp.dot(d.astype(BF16), wb_ref[g], preferred_element_type=F32) * sc_ref[:, cols]
        y_ref[:, cols] = y.astype(y_ref.dtype)


def _pool_seq_kernel(u_ref, halo_ref, wg_ref, sc_ref, y_ref, wb_ref, *, tm, tiles_per_seq):
    i = pl.program_id(0)

    @pl.when(i == 0)
    def _cast():
        wb_ref[...] = wg_ref[...].astype(BF16)

    t_in_seq = i % tiles_per_seq
    halo = jnp.where(t_in_seq == 0, 0.0, halo_ref[...])
    pos = t_in_seq * tm + lax.broadcasted_iota(jnp.int32, (tm, 1), 0)

    def window_sum(g, w, cols):
        s = jnp.concatenate([halo[:, cols], u_ref[:, cols]], axis=0)
        k = 1
        while k < w:
            s = s + pltpu.roll(s, k, 0)
            k *= 2
        return s[HALO:]

    def inv_cnt(w):
        return 1.0 / jnp.minimum(pos + 1, w).astype(F32)

    _pool_groups(window_sum, lambda cols: u_ref[:, cols], inv_cnt, wb_ref, sc_ref, y_ref)


def _pool_seq(h, w_pool_grp, pool_scale, layer, *, tm):
    m = h.shape[1]
    tiles_per_seq = SEQ // tm
    return pl.pallas_call(
        functools.partial(_pool_seq_kernel, tm=tm, tiles_per_seq=tiles_per_seq),
        grid=(m // tm,),
        in_specs=[pl.BlockSpec((None, tm, POOL_W), lambda i: (layer, i, 0)),
                  pl.BlockSpec((None, HALO, POOL_W),
                               lambda i: (layer, jnp.maximum(i * (tm // HALO) - 1, 0), 0)),
                  pl.BlockSpec((None, len(POOL_WINDOWS), POOL_GC, POOL_GC), lambda i: (layer, 0, 0, 0)),
                  pl.BlockSpec((None, 1, POOL_W), lambda i: (layer, 0, 0))],
        out_specs=pl.BlockSpec((tm, POOL_W), lambda i: (i, 0)),
        out_shape=jax.ShapeDtypeStruct((m, POOL_W), BF16),
        scratch_shapes=[pltpu.VMEM((len(POOL_WINDOWS), POOL_GC, POOL_GC), BF16)],
        compiler_params=_params(1),
        name="pool_seq",
    )(h, h, w_pool_grp, pool_scale)


def _pool_dec_kernel(u_ref, st_ref, wg_ref, sc_ref, y_ref, wb_ref):
    wb_ref[...] = wg_ref[...].astype(BF16)

    def window_sum(g, w, cols):
        s = u_ref[:, cols]
        for k in range(1, w):
            s = s + st_ref[POOL_CTX - k, :, cols]
        return s

    def inv_cnt(w):
        return 1.0 / float(min(PAST_LEN + 1, w))

    _pool_groups(window_sum, lambda cols: u_ref[:, cols], inv_cnt, wb_ref, sc_ref, y_ref)


def _pool_dec(h, st, w_pool_grp, pool_scale, layer):
    m = h.shape[1]
    return pl.pallas_call(
        _pool_dec_kernel,
        grid=(1,),
        in_specs=[pl.BlockSpec((None, m, POOL_W), lambda i: (layer, 0, 0)),
                  pl.BlockSpec((None, POOL_CTX, m, POOL_W), lambda i: (layer, 0, 0, 0)),
                  pl.BlockSpec((None, len(POOL_WINDOWS), POOL_GC, POOL_GC), lambda i: (layer, 0, 0, 0)),
                  pl.BlockSpec((None, 1, POOL_W), lambda i: (layer, 0, 0))],
        out_specs=pl.BlockSpec((m, POOL_W), lambda i: (0, 0)),
        out_shape=jax.ShapeDtypeStruct((m, POOL_W), BF16),
        scratch_shapes=[pltpu.VMEM((len(POOL_WINDOWS), POOL_GC, POOL_GC), BF16)],
        compiler_params=_params(1),
        name="pool_dec",
    )(h, st, w_pool_grp, pool_scale)


def _merge_groups(o_list, lse_list):
    mx = functools.reduce(jnp.maximum, lse_list)
    ws = [jnp.exp(l - mx) for l in lse_list]
    num = functools.reduce(lambda a, b: a + b, [w * o for w, o in zip(ws, o_list)])
    den = functools.reduce(lambda a, b: a + b, ws)
    return num / den


def _attn_seq_kernel(*refs):
    qkv = refs[:9]
    o_ref = refs[9]
    og_refs = refs[10:13]
    lg_refs = refs[13:16]

    qi = lax.broadcasted_iota(jnp.int32, (ATT_BLK, ATT_BLK), 0)
    kj = lax.broadcasted_iota(jnp.int32, (ATT_BLK, ATT_BLK), 1)
    cur_mask = kj <= qi
    qi2 = lax.broadcasted_iota(jnp.int32, (ATT_BLK, 2 * ATT_BLK), 0)
    kj2 = lax.broadcasted_iota(jnp.int32, (ATT_BLK, 2 * ATT_BLK), 1)
    both_mask = (kj2 >= qi2) & (kj2 - qi2 <= ATT_BLK)

    for g, (win, dil) in enumerate(ATT_GROUPS):
        q_ref, k_ref, v_ref = qkv[3 * g:3 * g + 3]
        n_blk = SEQ // dil // ATT_BLK
        for r in range(dil):
            for nb in range(n_blk):
                q_rows = pl.ds(r + nb * ATT_BLK * dil, ATT_BLK, stride=dil)
                if nb == 0:
                    kv_rows, mask = q_rows, cur_mask
                else:
                    kv_rows = pl.ds(r + (nb - 1) * ATT_BLK * dil, 2 * ATT_BLK, stride=dil)
                    mask = both_mask
                q = q_ref[q_rows, :].astype(BF16)
                k = k_ref[kv_rows, :].astype(BF16)
                v = v_ref[kv_rows, :].astype(BF16)
                s = lax.dot_general(q, k, (((1,), (1,)), ((), ())), preferred_element_type=F32) * SCALE
                s = jnp.where(mask, s, NEG)
                mx = jnp.max(s, axis=-1, keepdims=True)
                p = jnp.exp(s - mx)
                den = jnp.sum(p, axis=-1, keepdims=True)
                o = jnp.dot(p.astype(BF16), v, preferred_element_type=F32) / den
                og_refs[g][q_rows, :] = o
                lg_refs[g][q_rows, :] = jnp.broadcast_to(mx + jnp.log(den), (ATT_BLK, HEAD_DIM))

    chunk = 256
    for c in range(SEQ // chunk):
        rows = pl.ds(c * chunk, chunk)
        out = _merge_groups([r_[rows, :] for r_ in og_refs], [r_[rows, :] for r_ in lg_refs])
        o_ref[rows, :] = out.astype(o_ref.dtype)


def _attn_seq(h, layer):
    m = h.shape[1]
    in_specs = []
    for g in range(N_GROUPS):
        for which in range(3):
            blk0 = (POOL_W + (3 * g + which) * ATT_W) // HEAD_DIM
            in_specs.append(pl.BlockSpec((None, SEQ, HEAD_DIM),
                                         lambda b, hd, blk0=blk0: (layer, b, blk0 + hd)))
    return pl.pallas_call(
        _attn_seq_kernel,
        grid=(m // SEQ, N_HEADS),
        in_specs=in_specs,
        out_specs=pl.BlockSpec((SEQ, HEAD_DIM), lambda b, hd: (b, hd)),
        out_shape=jax.ShapeDtypeStruct((m, ATT_W), BF16),
        scratch_shapes=[pltpu.VMEM((SEQ, HEAD_DIM), F32)] * (2 * N_GROUPS),
        compiler_params=_params(2),
        name="attn_seq",
    )(*([h] * 9))


def _attn_dec_kernel(h_ref, c1_ref, c2_ref, c3_ref, o_ref):
    def rnd(t):
        return t.astype(BF16).astype(F32)

    outs, lses = [], []
    for g, c_ref in enumerate((c1_ref, c2_ref, c3_ref)):
        r0 = (POOL_W + 3 * g * ATT_W) // HEAD_DIM
        q = rnd(h_ref[r0:r0 + N_HEADS, :])
        k_new = rnd(h_ref[r0 + N_HEADS:r0 + 2 * N_HEADS, :])
        v_new = rnd(h_ref[r0 + 2 * N_HEADS:r0 + 3 * N_HEADS, :])
        kc = rnd(c_ref[:, 0])
        vc = rnd(c_ref[:, 1])
        s = jnp.sum(kc * q[None], axis=-1, keepdims=True) * SCALE
        s_new = jnp.sum(k_new * q, axis=-1, keepdims=True) * SCALE
        mx = jnp.maximum(jnp.max(s, axis=0), s_new)
        p = jnp.exp(s - mx[None])
        p_new = jnp.exp(s_new - mx)
        den = jnp.sum(p, axis=0) + p_new
        acc = jnp.sum(rnd(p) * vc, axis=0) + rnd(p_new) * v_new
        outs.append(acc / den)
        lses.append(mx + jnp.log(den))
    o_ref[...] = _merge_groups(outs, lses)


def _attn_dec(h4, caches, layer):
    in_specs = [pl.BlockSpec((None, None, IN_W // HEAD_DIM, HEAD_DIM), lambda b: (layer, b, 0, 0))]
    for c in caches:
        in_specs.append(pl.BlockSpec((None, None, c.shape[2], None, 2, N_HEADS, HEAD_DIM),
                                     lambda b: (layer, b, 0, 0, 0, 0, 0)))
    return pl.pallas_call(
        _attn_dec_kernel,
        grid=(DEC_BATCH,),
        in_specs=in_specs,
        out_specs=pl.BlockSpec((None, N_HEADS, HEAD_DIM), lambda b: (b, 0, 0)),
        out_shape=jax.ShapeDtypeStruct((DEC_BATCH, N_HEADS, HEAD_DIM), F32),
        compiler_params=_params(1),
        name="attn_dec",
    )(h4, *caches)


def _gate_kernel(yp_ref, oa_ref, wp_ref, wa_ref, gp_ref, ga_ref, o_ref, wpb_ref, wab_ref):
    @pl.when(pl.program_id(1) == 0)
    def _cast():
        wpb_ref[...] = wp_ref[...].astype(BF16)
        wab_ref[...] = wa_ref[...].astype(BF16)

    bp = jnp.dot(yp_ref[...], wpb_ref[...], preferred_element_type=F32)
    ba = jnp.dot(oa_ref[...], wab_ref[...], preferred_element_type=F32)
    merged = jax.nn.sigmoid(gp_ref[...]) * bp + jax.nn.sigmoid(ga_ref[...]) * ba
    o_ref[...] = merged.astype(o_ref.dtype)


def _gate_merge(y_pool, o_att, h, w_br_pool, w_br_att, layer, *, tm, tn):
    m = h.shape[1]
    gp0 = GATE_OFF // tn
    ga0 = (GATE_OFF + D_MODEL) // tn
    return pl.pallas_call(
        _gate_kernel,
        grid=(D_MODEL // tn, m // tm),
        in_specs=[pl.BlockSpec((tm, POOL_W), lambda j, i: (i, 0)),
                  pl.BlockSpec((tm, ATT_W), lambda j, i: (i, 0)),
                  pl.BlockSpec((None, POOL_W, tn), lambda j, i: (layer, 0, j)),
                  pl.BlockSpec((None, ATT_W, tn), lambda j, i: (layer, 0, j)),
                  pl.BlockSpec((None, tm, tn), lambda j, i: (layer, i, gp0 + j)),
                  pl.BlockSpec((None, tm, tn), lambda j, i: (layer, i, ga0 + j))],
        out_specs=pl.BlockSpec((tm, tn), lambda j, i: (i, j)),
        out_shape=jax.ShapeDtypeStruct((m, D_MODEL), BF16),
        scratch_shapes=[pltpu.VMEM((POOL_W, tn), BF16), pltpu.VMEM((ATT_W, tn), BF16)],
        compiler_params=_params(2),
        name="gate_merge",
    )(y_pool, o_att, w_br_pool, w_br_att, h, h)


def _gelu_gate(a2, a1, a, b, cw_ref, cb_ref):
    ac = cw_ref[0:1] * a2 + cw_ref[1:2] * a1 + cw_ref[2:3] * a + cb_ref[...]
    gelu = 0.5 * ac * (1.0 + lax.erf(ac * (0.5 ** 0.5)))
    return gelu * b


def _ffn_up_seq_kernel(x_ref, wa_ref, wb_ref, cw_ref, cb_ref, hm_ref, tail_ref, wq_ref,
                       wab_ref, wbb_ref, carry_ref, sem, *, tm, tn, tiles_per_seq, n_chunks):
    j, i = pl.program_id(0), pl.program_id(1)
    n_panels = pl.num_programs(0)

    def panel_copies(jj):
        return (pltpu.make_async_copy(wab_ref, wq_ref.at[:, pl.ds(jj * tn, tn)], sem.at[0]),
                pltpu.make_async_copy(wbb_ref, wq_ref.at[:, pl.ds((n_panels + jj) * tn, tn)], sem.at[1]))

    @pl.when(i == 0)
    def _cast():
        @pl.when(j > 0)
        def _drain():
            for c in panel_copies(j - 1):
                c.wait()
        wab_ref[...] = wa_ref[...].astype(BF16)
        wbb_ref[...] = wb_ref[...].astype(BF16)
        for c in panel_copies(j):
            c.start()

    @pl.when(i % tiles_per_seq == 0)
    def _reset():
        carry_ref[...] = jnp.zeros_like(carry_ref)

    tc = tm // n_chunks
    row = lax.broadcasted_iota(jnp.int32, (tc, 1), 0)
    carry = carry_ref[...]
    for c in range(n_chunks):
        rows = pl.ds(c * tc, tc)
        x = x_ref[rows, :]
        a = jnp.dot(x, wab_ref[...], preferred_element_type=F32)
        b = jnp.dot(x, wbb_ref[...], preferred_element_type=F32)
        a1 = jnp.where(row == 0, carry[7:8], pltpu.roll(a, 1, 0))
        a2 = jnp.where(row == 0, carry[6:7], jnp.where(row == 1, carry[7:8], pltpu.roll(a, 2, 0)))
        hm_ref[rows, :] = _gelu_gate(a2, a1, a, b, cw_ref, cb_ref).astype(hm_ref.dtype)
        carry = a[tc - 8:]
    carry_ref[...] = carry
    tail_ref[0] = carry

    @pl.when((j == n_panels - 1) & (i == pl.num_programs(1) - 1))
    def _last_wait():
        for c in panel_copies(j):
            c.wait()


def _ffn_up_seq(x, w_up, conv_w, conv_b, layer, *, tm, tn, n_chunks):
    m, k = x.shape
    n_panels = D_FF // tn
    return pl.pallas_call(
        functools.partial(_ffn_up_seq_kernel, tm=tm, tn=tn, tiles_per_seq=SEQ // tm, n_chunks=n_chunks),
        grid=(n_panels, m // tm),
        in_specs=[pl.BlockSpec((tm, k), lambda j, i: (i, 0)),
                  pl.BlockSpec((None, k, tn), lambda j, i: (layer, 0, j)),
                  pl.BlockSpec((None, k, tn), lambda j, i: (layer, 0, n_panels + j)),
                  pl.BlockSpec((None, 3, tn), lambda j, i: (layer, 0, j)),
                  pl.BlockSpec((None, 1, tn), lambda j, i: (layer, 0, j))],
        out_specs=[pl.BlockSpec((tm, tn), lambda j, i: (i, j)),
                   pl.BlockSpec((1, 8, tn), lambda j, i: (i, 0, j)),
                   pl.BlockSpec(memory_space=pl.ANY)],
        out_shape=[jax.ShapeDtypeStruct((m, D_FF), BF16),
                   jax.ShapeDtypeStruct((m // tm, 8, D_FF), F32),
                   jax.ShapeDtypeStruct((k, 2 * D_FF), BF16)],
        scratch_shapes=[pltpu.VMEM((k, tn), BF16), pltpu.VMEM((k, tn), BF16),
                        pltpu.VMEM((8, tn), F32), pltpu.SemaphoreType.DMA((2,))],
        compiler_params=_params(2),
        name="ffn_up",
    )(x, w_up, w_up, conv_w, conv_b)


def _ffn_up_dec_kernel(x_ref, wa_ref, wb_ref, cw_ref, cb_ref, p2_ref, p1_ref, hm_ref, a_ref):
    x = x_ref[...]
    a = jnp.dot(x, wa_ref[...], preferred_element_type=F32)
    b = jnp.dot(x, wb_ref[...], preferred_element_type=F32)
    a_ref[...] = a
    hm_ref[...] = _gelu_gate(p2_ref[...], p1_ref[...], a, b, cw_ref, cb_ref).astype(hm_ref.dtype)


def _ffn_up_dec(x, wq_up, conv_w, conv_b, layer, prev, *, tn):
    m, k = x.shape
    n_panels = D_FF // tn
    return pl.pallas_call(
        _ffn_up_dec_kernel,
        grid=(n_panels,),
        in_specs=[pl.BlockSpec((m, k), lambda j: (0, 0)),
                  pl.BlockSpec((k, tn), lambda j: (0, j)),
                  pl.BlockSpec((k, tn), lambda j: (0, n_panels + j)),
                  pl.BlockSpec((None, 3, tn), lambda j: (layer, 0, j)),
                  pl.BlockSpec((None, 1, tn), lambda j: (layer, 0, j)),
                  pl.BlockSpec((None, m, tn), lambda j: (0, 0, j)),
                  pl.BlockSpec((None, m, tn), lambda j: (1, 0, j))],
        out_specs=[pl.BlockSpec((m, tn), lambda j: (0, j)), pl.BlockSpec((m, tn), lambda j: (0, j))],
        out_shape=[jax.ShapeDtypeStruct((m, D_FF), BF16), jax.ShapeDtypeStruct((m, D_FF), F32)],
        compiler_params=_params(1),
        name="ffn_up_dec",
    )(x, wq_up, wq_up, conv_w, conv_b, prev, prev)


def _layer(resid, xb, h_prev, layer, p, *, dec=None):
    m = xb.shape[0]
    t = _tiles(m)
    out = {}
    if dec is None:
        tm_in, tn_in = t['proj_first'] if h_prev is None else t['proj']
        h, out['wq_in'] = _proj(xb, p['w_in'], layer, n=IN_W, tm=tm_in, tn=tn_in,
                                stack=(DEPTH, h_prev), emit_bf16=True, name="in_proj")
    else:
        h = _proj_q(xb, dec['wq_in'][layer], tn=t['proj'][1], layer=layer, stack=(DEPTH, h_prev),
                    name="in_proj_dec")
    if dec is None:
        y_pool = _pool_seq(h, p['w_pool_grp'], p['pool_scale'], layer, tm=t['pool'])
        o_att = _attn_seq(h, layer)
    else:
        y_pool = _pool_dec(h, dec['pool'], p['w_pool_grp'], p['pool_scale'], layer)
        h4 = h.reshape(DEPTH, m, IN_W // HEAD_DIM, HEAD_DIM)
        o = _attn_dec(h4, dec['caches'], layer).reshape(DEC_BATCH, ATT_W)
        o_att = jnp.pad(o, ((0, m - DEC_BATCH), (0, 0))).astype(BF16)
    merged = _gate_merge(y_pool, o_att, h, p['w_br_pool'], p['w_br_att'], layer,
                         tm=t['gate'][0], tn=t['gate'][1])
    z = _proj(merged, p['w_out'], layer, n=D_MODEL, tm=t['proj_first'][0], tn=t['proj_first'][1],
              resid=resid, alpha=ALPHA, name="out_proj")
    x1b, mu1, rstd1 = _layer_norm(z, p['ln1_g'], p['ln1_b'], layer, tm=t['ln'])
    if dec is None:
        hm, tail, out['wq_up'] = _ffn_up_seq(x1b, p['w_up'], p['conv_w'], p['conv_b'], layer,
                                             tm=t['ffn_up'][0], tn=t['ffn_up'][1],
                                             n_chunks=t['ffn_chunks'])
    else:
        hm, tail = _ffn_up_dec(x1b, dec['wq_up'][layer], p['conv_w'], p['conv_b'], layer,
                               dec['conv'][layer], tn=t['ffn_up'][1])
    z2 = _down(hm, p['w_down'], layer, (z, mu1, rstd1, p['ln1_g'], p['ln1_b'], layer),
               tm=t['down'][0], tn=t['down'][1], tk=t['down'][2])
    if layer == DEPTH - 1:
        out['y'] = _layer_norm(z2, p['ln2_g'], p['ln2_b'], layer, tm=t['ln'], full=True)
    else:
        out['xb'], mu2, rstd2 = _layer_norm(z2, p['ln2_g'], p['ln2_b'], layer, tm=t['ln'])
        out['resid'] = (z2, mu2, rstd2, p['ln2_g'], p['ln2_b'], layer)
    out.update(h=h, tail=tail)
    return out


def _run_prompt(x_prompt, p):
    batch = x_prompt.shape[0]
    resid = x_prompt.reshape(batch * SEQ, D_MODEL)
    xb = resid.astype(BF16)
    h, tails, wq_in, wq_up = None, [], [], []
    for l in range(DEPTH):
        o = _layer(resid, xb, h, l, p)
        h, resid, xb = o['h'], o.get('resid'), o.get('xb')
        tails.append(o['tail'])
        wq_in.append(o['wq_in'])
        wq_up.append(o['wq_up'])
    x = o['y']
    h4 = h.reshape(DEPTH, batch, SEQ, IN_W)
    pool = h4[:, :, SEQ - POOL_CTX:, :POOL_W]
    kvs = []
    for g, (win, _) in enumerate(ATT_GROUPS):
        base = POOL_W + 3 * g * ATT_W + ATT_W
        keep = min(win, SEQ)
        kvs.append(h4[:, :, SEQ - keep:, base:base + 2 * ATT_W]
                   .reshape(DEPTH, batch, keep, 2, N_HEADS, HEAD_DIM))
    tm = _tiles(batch * SEQ)['ffn_up'][0]
    t5 = jnp.stack(tails).reshape(DEPTH, batch, SEQ // tm, 8, D_FF)
    conv = t5[:, :, SEQ // tm - 1, 6:8, :]
    return (x.reshape(batch, SEQ, D_MODEL), pool, *kvs, conv), wq_in, wq_up


def _run_sample(x_sample, state_pool, caches_in, state_conv, p, wq_in, wq_up):
    pad = DEC_ROWS - DEC_BATCH
    resid = jnp.pad(x_sample.reshape(DEC_BATCH, D_MODEL), ((0, pad), (0, 0)))
    dec = dict(
        wq_in=wq_in, wq_up=wq_up,
        pool=jnp.pad(jnp.transpose(state_pool, (0, 2, 1, 3)), ((0, 0), (0, 0), (0, pad), (0, 0))),
        conv=jnp.pad(jnp.transpose(state_conv, (0, 2, 1, 3)), ((0, 0), (0, 0), (0, pad), (0, 0))),
        caches=[c.reshape(DEPTH, DEC_BATCH, c.shape[2] // dil, dil, 2, N_HEADS, HEAD_DIM)
                for c, (_, dil) in zip(caches_in, ATT_GROUPS)],
    )
    xb = resid.astype(BF16)
    h, a_new = None, []
    for l in range(DEPTH):
        o = _layer(resid, xb, h, l, p, dec=dec)
        h, resid, xb = o['h'], o.get('resid'), o.get('xb')
        a_new.append(o['tail'][:DEC_BATCH])
    x = o['y']
    h = h[:, :DEC_BATCH]
    pool = jnp.concatenate([state_pool[:, :, 1:], h[:, :, None, :POOL_W]], axis=2)
    kvs = []
    for g, (win, _) in enumerate(ATT_GROUPS):
        base = POOL_W + 3 * g * ATT_W + ATT_W
        kv_new = h[:, :, base:base + 2 * ATT_W].reshape(DEPTH, DEC_BATCH, 1, 2, N_HEADS, HEAD_DIM)
        keep = min(win, PAST_LEN + 1)
        kvs.append(jnp.concatenate([caches_in[g], kv_new], axis=2)[:, :, caches_in[g].shape[2] + 1 - keep:])
    conv = jnp.concatenate([state_conv[:, :, 1:], jnp.stack(a_new)[:, :, None, :]], axis=2)
    return (x[:DEC_BATCH].reshape(DEC_BATCH, 1, D_MODEL), pool, *kvs, conv)


def kernel(x_prompt, x_sample, state_pool, cache_kv1, cache_kv2, cache_kv3, state_conv, w_in, w_pool_grp, pool_scale, w_br_pool, w_br_att, w_out, ln1_g, ln1_b, w_up, conv_w, conv_b, w_down, ln2_g, ln2_b):
    def row3(t):
        return t.reshape(DEPTH, 1, t.shape[-1])

    p = dict(w_in=w_in, w_pool_grp=w_pool_grp, pool_scale=row3(pool_scale), w_br_pool=w_br_pool,
             w_br_att=w_br_att, w_out=w_out, ln1_g=row3(ln1_g), ln1_b=row3(ln1_b), w_up=w_up,
             conv_w=conv_w, conv_b=row3(conv_b), w_down=w_down, ln2_g=row3(ln2_g), ln2_b=row3(ln2_b))
    (yp, pool_p, kv1_p, kv2_p, kv3_p, conv_p), wq_in, wq_up = _run_prompt(x_prompt, p)
    ys, pool_s, kv1_s, kv2_s, kv3_s, conv_s = _run_sample(
        x_sample, state_pool, (cache_kv1, cache_kv2, cache_kv3), state_conv, p, wq_in, wq_up)
    return (yp, ys, pool_p, pool_s, kv1_p, kv1_s, kv2_p, kv2_s, kv3_p, kv3_s, conv_p, conv_s)
```

```python
import functools

import jax
import jax.numpy as jnp
from jax import lax
from jax.experimental import pallas as pl
from jax.experimental.pallas import tpu as pltpu

D_MODEL = 4096
SEQ = 2048
DEPTH = 2
DEC_BATCH = 8
PAST_LEN = 16384
HEAD_DIM = 128
N_HEADS = 8
ATT_GROUPS = ((128, 1), (512, 4), (2048, 16))
N_GROUPS = len(ATT_GROUPS)
ATT_W = N_HEADS * HEAD_DIM
POOL_WINDOWS = (2, 4, 8, 16)
POOL_W = D_MODEL // 2
POOL_GC = POOL_W // len(POOL_WINDOWS)
POOL_CTX = max(POOL_WINDOWS) - 1
D_FF = 11008
GATE_OFF = POOL_W + 3 * N_GROUPS * ATT_W
IN_W = GATE_OFF + 2 * D_MODEL
ALPHA = (2.0 * DEPTH) ** 0.25
LN_EPS = 1e-5
NEG = -1e30
ATT_BLK = 128
SCALE = HEAD_DIM ** -0.5

LANES = 128
DEC_ROWS = 16
HALO = 16
VMEM_LIMIT = 64 * 1024 * 1024

F32 = jnp.float32
BF16 = jnp.bfloat16

assert all(PAST_LEN >= win and win % dil == 0 for win, dil in ATT_GROUPS)


def _tiles(m):
    big = m >= 1024
    return dict(
        proj=(1024 if big else m, 1024),
        proj_first=(512 if big else m, 1024),
        gate=(1024 if big else m, 1024),
        ffn_up=(2048 if big else m, 256),
        ffn_chunks=4,
        down=(2048, 1024, 1024) if big else (m, 1024, 2048),
        ln=512 if big else m,
        pool=512,
    )


def _params(n_axes, flags=None):
    return pltpu.CompilerParams(dimension_semantics=("arbitrary",) * n_axes,
                                vmem_limit_bytes=VMEM_LIMIT, flags=flags)


def _resid_operands(resid, tm, tn, at):
    if not isinstance(resid, tuple):
        return [pl.BlockSpec((tm, tn), lambda *g: at(*g))], [resid]
    z, mu, rstd, g3, b3, ln_layer = resid
    specs = [pl.BlockSpec((tm, tn), lambda *g: at(*g)),
             pl.BlockSpec((tm, LANES), lambda *g: (at(*g)[0], 0)),
             pl.BlockSpec((tm, LANES), lambda *g: (at(*g)[0], 0)),
             pl.BlockSpec((None, 1, tn), lambda *g: (ln_layer, 0, at(*g)[1])),
             pl.BlockSpec((None, 1, tn), lambda *g: (ln_layer, 0, at(*g)[1]))]
    return specs, [z, mu, rstd, g3, b3]


def _resid_value(r_refs):
    if len(r_refs) == 1:
        return r_refs[0][...]
    z_ref, mu_ref, rstd_ref, g_ref, b_ref = r_refs
    mu, rstd = mu_ref[...], rstd_ref[...]
    parts = []
    for c in range(z_ref.shape[1] // LANES):
        cols = slice(c * LANES, (c + 1) * LANES)
        parts.append((z_ref[:, cols] - mu) * rstd * g_ref[:, cols] + b_ref[:, cols])
    return jnp.concatenate(parts, axis=1)


def _proj_kernel(*refs, alpha, n_resid, has_alias, emit_bf16, stream_w, layer, tn):
    refs = list(refs)
    x_ref, w_ref = refs[:2]
    r_refs = refs[2:2 + n_resid]
    n_in = 2 + n_resid + int(has_alias)
    o_ref = refs[n_in]
    wq_ref = refs[n_in + 1] if emit_bf16 else None
    scratch = refs[n_in + 1 + int(emit_bf16):]
    wb_ref = scratch[0]
    wf_ref, wf_sem = (scratch[1], scratch[2]) if stream_w else (None, None)
    sem = refs[-1] if emit_bf16 else None
    j, i = pl.program_id(0), pl.program_id(1)
    n_j = pl.num_programs(0)

    def panel_copy(jj):
        return pltpu.make_async_copy(wb_ref, wq_ref.at[:, pl.ds(jj * tn, tn)], sem)

    def panel_fetch(jj):
        return pltpu.make_async_copy(w_ref.at[layer, :, pl.ds(jj * tn, tn)], wf_ref, wf_sem)

    if stream_w:
        @pl.when((j == 0) & (i == 0))
        def _first_fetch():
            panel_fetch(0).start()

    @pl.when(i == 0)
    def _cast():
        if emit_bf16:
            @pl.when(j > 0)
            def _drain():
                panel_copy(j - 1).wait()
        if stream_w:
            panel_fetch(j).wait()
            wb_ref[...] = wf_ref[...].astype(BF16)
        else:
            wb_ref[...] = w_ref[...].astype(BF16)
        if emit_bf16:
            panel_copy(j).start()

    if stream_w:
        @pl.when((i == 1) & (j + 1 < n_j))
        def _next_fetch():
            panel_fetch(j + 1).start()

    acc = jnp.dot(x_ref[...], wb_ref[...], preferred_element_type=F32)
    if n_resid:
        acc = alpha * _resid_value(r_refs) + acc
    if len(o_ref.shape) == 3:
        o_ref[0] = acc
        o_ref[1:] = jnp.zeros((o_ref.shape[0] - 1,) + acc.shape, o_ref.dtype)
    else:
        o_ref[...] = acc.astype(o_ref.dtype)

    if emit_bf16:
        @pl.when((j == pl.num_programs(0) - 1) & (i == pl.num_programs(1) - 1))
        def _last_wait():
            panel_copy(j).wait()


def _proj(x, w3, layer, *, n, tm, tn, resid=None, alpha=1.0, stack=None, emit_bf16=False, name):
    m, k = x.shape
    grid = (n // tn, m // tm)
    stream_w = grid[1] > 1
    w_spec = (pl.BlockSpec(memory_space=pl.ANY) if stream_w
              else pl.BlockSpec((None, k, tn), lambda j, i: (layer, 0, j)))
    in_specs = [pl.BlockSpec((tm, k), lambda j, i: (i, 0)), w_spec]
    args = [x, w3]
    n_resid = 0
    if resid is not None:
        r_specs, r_args = _resid_operands(resid, tm, tn, lambda j, i: (i, j))
        in_specs += r_specs
        args += r_args
        n_resid = len(r_args)
    aliases = {}
    if stack is None:
        out_shapes = [jax.ShapeDtypeStruct((m, n), F32)]
        out_specs = [pl.BlockSpec((tm, tn), lambda j, i: (i, j))]
    else:
        depth, prev = stack
        out_shapes = [jax.ShapeDtypeStruct((depth, m, n), F32)]
        out_specs = [pl.BlockSpec((None, tm, tn), lambda j, i: (layer, i, j))]
        if prev is None:
            assert layer == 0
            out_specs = [pl.BlockSpec((depth, tm, tn), lambda j, i: (0, i, j))]
        else:
            in_specs.append(pl.BlockSpec(memory_space=pl.ANY))
            aliases = {len(args): 0}
            args.append(prev)
    scratch = [pltpu.VMEM((k, tn), BF16)]
    if stream_w:
        scratch += [pltpu.VMEM((k, tn), F32), pltpu.SemaphoreType.DMA(())]
    if emit_bf16:
        out_shapes.append(jax.ShapeDtypeStruct((k, n), BF16))
        out_specs.append(pl.BlockSpec(memory_space=pl.ANY))
        scratch.append(pltpu.SemaphoreType.DMA(()))
    outs = pl.pallas_call(
        functools.partial(_proj_kernel, alpha=alpha, n_resid=n_resid, has_alias=bool(aliases),
                          emit_bf16=emit_bf16, stream_w=stream_w, layer=layer, tn=tn),
        grid=grid,
        in_specs=in_specs,
        out_specs=out_specs,
        out_shape=out_shapes,
        scratch_shapes=scratch,
        input_output_aliases=aliases,
        compiler_params=_params(2),
        name=name,
    )(*args)
    return tuple(outs) if emit_bf16 else outs[0]


def _proj_q(x, wq, *, tn, layer, stack, name):
    m, k = x.shape
    n = wq.shape[1]
    depth, prev = stack
    in_specs = [pl.BlockSpec((m, k), lambda j: (0, 0)), pl.BlockSpec((k, tn), lambda j: (0, j))]
    args = [x, wq]
    aliases = {}
    if prev is None:
        assert layer == 0
        out_spec = pl.BlockSpec((depth, m, tn), lambda j: (0, 0, j))
        kern = _proj_q_first_kernel
    else:
        out_spec = pl.BlockSpec((None, m, tn), lambda j: (layer, 0, j))
        in_specs.append(pl.BlockSpec(memory_space=pl.ANY))
        aliases = {2: 0}
        args.append(prev)
        kern = _proj_q_next_kernel
    return pl.pallas_call(
        kern,
        grid=(n // tn,),
        in_specs=in_specs,
        out_specs=out_spec,
        out_shape=jax.ShapeDtypeStruct((depth, m, n), F32),
        input_output_aliases=aliases,
        compiler_params=_params(1),
        name=name,
    )(*args)


def _proj_q_first_kernel(x_ref, w_ref, o_ref):
    o_ref[0] = jnp.dot(x_ref[...], w_ref[...], preferred_element_type=F32)
    o_ref[1:] = jnp.zeros((o_ref.shape[0] - 1,) + o_ref.shape[1:], o_ref.dtype)


def _proj_q_next_kernel(x_ref, w_ref, prev_ref, o_ref):
    del prev_ref
    o_ref[...] = jnp.dot(x_ref[...], w_ref[...], preferred_element_type=F32)


def _down_kernel(*refs, alpha, rem):
    x_ref, w_ref = refs[:2]
    r_refs = refs[2:-1]
    o_ref = refs[-1]
    kk = pl.program_id(2)
    nk = pl.num_programs(2)

    @pl.when(kk == 0)
    def _first():
        o_ref[...] = alpha * _resid_value(r_refs) + jnp.dot(x_ref[...], w_ref[...].astype(BF16),
                                                            preferred_element_type=F32)

    @pl.when((kk > 0) & (kk < nk - 1))
    def _full():
        o_ref[...] += jnp.dot(x_ref[...], w_ref[...].astype(BF16), preferred_element_type=F32)

    @pl.when(kk == nk - 1)
    def _last():
        o_ref[...] += jnp.dot(x_ref[:, :rem], w_ref[:rem, :].astype(BF16), preferred_element_type=F32)


def _down(x, w3, layer, resid, *, tm, tn, tk):
    m, k = x.shape
    n = w3.shape[2]
    nk = pl.cdiv(k, tk)
    assert nk >= 2
    rem = k - (nk - 1) * tk
    r_specs, r_args = _resid_operands(resid, tm, tn, lambda j, i, kk: (i, j))
    return pl.pallas_call(
        functools.partial(_down_kernel, alpha=ALPHA, rem=rem),
        grid=(n // tn, m // tm, nk),
        in_specs=[pl.BlockSpec((tm, tk), lambda j, i, kk: (i, kk)),
                  pl.BlockSpec((None, tk, tn), lambda j, i, kk: (layer, kk, j))] + r_specs,
        out_specs=pl.BlockSpec((tm, tn), lambda j, i, kk: (i, j)),
        out_shape=jax.ShapeDtypeStruct((m, n), F32),
        compiler_params=_params(3),
        name="down_proj",
    )(x, w3, *r_args)


def _ln_kernel(z_ref, g_ref, b_ref, *o_refs, full):
    z = z_ref[...]
    mu = jnp.mean(z, axis=-1, keepdims=True)
    zc = z - mu
    var = jnp.mean(zc * zc, axis=-1, keepdims=True)
    rstd = lax.rsqrt(var + LN_EPS)
    y = zc * rstd * g_ref[...] + b_ref[...]
    if full:
        o_refs[0][...] = y
    else:
        ob_ref, mu_ref, rstd_ref = o_refs
        ob_ref[...] = y.astype(BF16)
        mu_ref[...] = jnp.broadcast_to(mu, mu_ref.shape)
        rstd_ref[...] = jnp.broadcast_to(rstd, rstd_ref.shape)


def _layer_norm(z, g3, b3, layer, *, tm, full=False):
    m, d = z.shape
    row = pl.BlockSpec((tm, d), lambda i: (i, 0))
    col = pl.BlockSpec((tm, LANES), lambda i: (i, 0))
    if full:
        out_specs, out_shape = [row], [jax.ShapeDtypeStruct((m, d), F32)]
    else:
        out_specs = [row, col, col]
        out_shape = [jax.ShapeDtypeStruct((m, d), BF16), jax.ShapeDtypeStruct((m, LANES), F32),
                     jax.ShapeDtypeStruct((m, LANES), F32)]
    outs = pl.pallas_call(
        functools.partial(_ln_kernel, full=full),
        grid=(m // tm,),
        in_specs=[row,
                  pl.BlockSpec((None, 1, d), lambda i: (layer, 0, 0)),
                  pl.BlockSpec((None, 1, d), lambda i: (layer, 0, 0))],
        out_specs=out_specs,
        out_shape=out_shape,
        compiler_params=_params(1),
        name="layer_norm",
    )(z, g3, b3)
    return outs[0] if full else tuple(outs)


def _pool_groups(window_sum_fn, u_fn, inv_cnt_fn, wb_ref, sc_ref, y_ref):
    for g, w in enumerate(POOL_WINDOWS):
        cols = slice(g * POOL_GC, (g + 1) * POOL_GC)
        d = window_sum_fn(g, w, cols) * inv_cnt_fn(w) - u_fn(cols)
        y = jnp.dot(d.astype(BF16), wb_ref[g], preferred_element_type=F32) * sc_ref[:, cols]
        y_ref[:, cols] = y.astype(y_ref.dtype)


def _pool_seq_kernel(u_ref, halo_ref, wg_ref, sc_ref, y_ref, wb_ref, *, tm, tiles_per_seq):
    i = pl.program_id(0)

    @pl.when(i == 0)
    def _cast():
        wb_ref[...] = wg_ref[...].astype(BF16)

    t_in_seq = i % tiles_per_seq
    halo = jnp.where(t_in_seq == 0, 0.0, halo_ref[...])
    pos = t_in_seq * tm + lax.broadcasted_iota(jnp.int32, (tm, 1), 0)

    def window_sum(g, w, cols):
        s = jnp.concatenate([halo[:, cols], u_ref[:, cols]], axis=0)
        k = 1
        while k < w:
            s = s + pltpu.roll(s, k, 0)
            k *= 2
        return s[HALO:]

    def inv_cnt(w):
        return 1.0 / jnp.minimum(pos + 1, w).astype(F32)

    _pool_groups(window_sum, lambda cols: u_ref[:, cols], inv_cnt, wb_ref, sc_ref, y_ref)


def _pool_seq(h, w_pool_grp, pool_scale, layer, *, tm):
    m = h.shape[1]
    tiles_per_seq = SEQ // tm
    return pl.pallas_call(
        functools.partial(_pool_seq_kernel, tm=tm, tiles_per_seq=tiles_per_seq),
        grid=(m // tm,),
        in_specs=[pl.BlockSpec((None, tm, POOL_W), lambda i: (layer, i, 0)),
                  pl.BlockSpec((None, HALO, POOL_W),
                               lambda i: (layer, jnp.maximum(i * (tm // HALO) - 1, 0), 0)),
                  pl.BlockSpec((None, len(POOL_WINDOWS), POOL_GC, POOL_GC), lambda i: (layer, 0, 0, 0)),
                  pl.BlockSpec((None, 1, POOL_W), lambda i: (layer, 0, 0))],
        out_specs=pl.BlockSpec((tm, POOL_W), lambda i: (i, 0)),
        out_shape=jax.ShapeDtypeStruct((m, POOL_W), BF16),
        scratch_shapes=[pltpu.VMEM((len(POOL_WINDOWS), POOL_GC, POOL_GC), BF16)],
        compiler_params=_params(1),
        name="pool_seq",
    )(h, h, w_pool_grp, pool_scale)


def _pool_dec_kernel(u_ref, st_ref, wg_ref, sc_ref, y_ref, wb_ref):
    wb_ref[...] = wg_ref[...].astype(BF16)

    def window_sum(g, w, cols):
        s = u_ref[:, cols]
        for k in range(1, w):
            s = s + st_ref[POOL_CTX - k, :, cols]
        return s

    def inv_cnt(w):
        return 1.0 / float(min(PAST_LEN + 1, w))

    _pool_groups(window_sum, lambda cols: u_ref[:, cols], inv_cnt, wb_ref, sc_ref, y_ref)


def _pool_dec(h, st, w_pool_grp, pool_scale, layer):
    m = h.shape[1]
    return pl.pallas_call(
        _pool_dec_kernel,
        grid=(1,),
        in_specs=[pl.BlockSpec((None, m, POOL_W), lambda i: (layer, 0, 0)),
                  pl.BlockSpec((None, POOL_CTX, m, POOL_W), lambda i: (layer, 0, 0, 0)),
                  pl.BlockSpec((None, len(POOL_WINDOWS), POOL_GC, POOL_GC), lambda i: (layer, 0, 0, 0)),
                  pl.BlockSpec((None, 1, POOL_W), lambda i: (layer, 0, 0))],
        out_specs=pl.BlockSpec((m, POOL_W), lambda i: (0, 0)),
        out_shape=jax.ShapeDtypeStruct((m, POOL_W), BF16),
        scratch_shapes=[pltpu.VMEM((len(POOL_WINDOWS), POOL_GC, POOL_GC), BF16)],
        compiler_params=_params(1),
        name="pool_dec",
    )(h, st, w_pool_grp, pool_scale)


def _merge_groups(o_list, lse_list):
    mx = functools.reduce(jnp.maximum, lse_list)
    ws = [jnp.exp(l - mx) for l in lse_list]
    num = functools.reduce(lambda a, b: a + b, [w * o for w, o in zip(ws, o_list)])
    den = functools.reduce(lambda a, b: a + b, ws)
    return num / den


def _attn_seq_kernel(*refs):
    qkv = refs[:9]
    o_ref = refs[9]
    og_refs = refs[10:13]
    lg_refs = refs[13:16]

    qi = lax.broadcasted_iota(jnp.int32, (ATT_BLK, ATT_BLK), 0)
    kj = lax.broadcasted_iota(jnp.int32, (ATT_BLK, ATT_BLK), 1)
    cur_mask = kj <= qi
    qi2 = lax.broadcasted_iota(jnp.int32, (ATT_BLK, 2 * ATT_BLK), 0)
    kj2 = lax.broadcasted_iota(jnp.int32, (ATT_BLK, 2 * ATT_BLK), 1)
    both_mask = (kj2 >= qi2) & (kj2 - qi2 <= ATT_BLK)

    for g, (win, dil) in enumerate(ATT_GROUPS):
        q_ref, k_ref, v_ref = qkv[3 * g:3 * g + 3]
        n_blk = SEQ // dil // ATT_BLK
        for r in range(dil):
            for nb in range(n_blk):
                q_rows = pl.ds(r + nb * ATT_BLK * dil, ATT_BLK, stride=dil)
                if nb == 0:
                    kv_rows, mask = q_rows, cur_mask
                else:
                    kv_rows = pl.ds(r + (nb - 1) * ATT_BLK * dil, 2 * ATT_BLK, stride=dil)
                    mask = both_mask
                q = q_ref[q_rows, :].astype(BF16)
                k = k_ref[kv_rows, :].astype(BF16)
                v = v_ref[kv_rows, :].astype(BF16)
                s = lax.dot_general(q, k, (((1,), (1,)), ((), ())), preferred_element_type=F32) * SCALE
                s = jnp.where(mask, s, NEG)
                mx = jnp.max(s, axis=-1, keepdims=True)
                p = jnp.exp(s - mx)
                den = jnp.sum(p, axis=-1, keepdims=True)
                o = jnp.dot(p.astype(BF16), v, preferred_element_type=F32) / den
                og_refs[g][q_rows, :] = o
                lg_refs[g][q_rows, :] = jnp.broadcast_to(mx + jnp.log(den), (ATT_BLK, HEAD_DIM))

    chunk = 256
    for c in range(SEQ // chunk):
        rows = pl.ds(c * chunk, chunk)
        out = _merge_groups([r_[rows, :] for r_ in og_refs], [r_[rows, :] for r_ in lg_refs])
        o_ref[rows, :] = out.astype(o_ref.dtype)


def _attn_seq(h, layer):
    m = h.shape[1]
    in_specs = []
    for g in range(N_GROUPS):
        for which in range(3):
            blk0 = (POOL_W + (3 * g + which) * ATT_W) // HEAD_DIM
            in_specs.append(pl.BlockSpec((None, SEQ, HEAD_DIM),
                                         lambda b, hd, blk0=blk0: (layer, b, blk0 + hd)))
    return pl.pallas_call(
        _attn_seq_kernel,
        grid=(m // SEQ, N_HEADS),
        in_specs=in_specs,
        out_specs=pl.BlockSpec((SEQ, HEAD_DIM), lambda b, hd: (b, hd)),
        out_shape=jax.ShapeDtypeStruct((m, ATT_W), BF16),
        scratch_shapes=[pltpu.VMEM((SEQ, HEAD_DIM), F32)] * (2 * N_GROUPS),
        compiler_params=_params(2),
        name="attn_seq",
    )(*([h] * 9))


def _attn_dec_kernel(h_ref, c1_ref, c2_ref, c3_ref, o_ref):
    def rnd(t):
        return t.astype(BF16).astype(F32)

    outs, lses = [], []
    for g, c_ref in enumerate((c1_ref, c2_ref, c3_ref)):
        r0 = (POOL_W + 3 * g * ATT_W) // HEAD_DIM
        q = rnd(h_ref[r0:r0 + N_HEADS, :])
        k_new = rnd(h_ref[r0 + N_HEADS:r0 + 2 * N_HEADS, :])
        v_new = rnd(h_ref[r0 + 2 * N_HEADS:r0 + 3 * N_HEADS, :])
        kc = rnd(c_ref[:, 0])
        vc = rnd(c_ref[:, 1])
        s = jnp.sum(kc * q[None], axis=-1, keepdims=True) * SCALE
        s_new = jnp.sum(k_new * q, axis=-1, keepdims=True) * SCALE
        mx = jnp.maximum(jnp.max(s, axis=0), s_new)
        p = jnp.exp(s - mx[None])
        p_new = jnp.exp(s_new - mx)
        den = jnp.sum(p, axis=0) + p_new
        acc = jnp.sum(rnd(p) * vc, axis=0) + rnd(p_new) * v_new
        outs.append(acc / den)
        lses.append(mx + jnp.log(den))
    o_ref[...] = _merge_groups(outs, lses)


def _attn_dec(h4, caches, layer):
    in_specs = [pl.BlockSpec((None, None, IN_W // HEAD_DIM, HEAD_DIM), lambda b: (layer, b, 0, 0))]
    for c in caches:
        in_specs.append(pl.BlockSpec((None, None, c.shape[2], None, 2, N_HEADS, HEAD_DIM),
                                     lambda b: (layer, b, 0, 0, 0, 0, 0)))
    return pl.pallas_call(
        _attn_dec_kernel,
        grid=(DEC_BATCH,),
        in_specs=in_specs,
        out_specs=pl.BlockSpec((None, N_HEADS, HEAD_DIM), lambda b: (b, 0, 0)),
        out_shape=jax.ShapeDtypeStruct((DEC_BATCH, N_HEADS, HEAD_DIM), F32),
        compiler_params=_params(1),
        name="attn_dec",
    )(h4, *caches)


def _panel_stream(fetches_of, cast):
    j, i = pl.program_id(0), pl.program_id(1)

    @pl.when((j == 0) & (i == 0))
    def _first_fetch():
        for c in fetches_of(0):
            c.start()

    @pl.when(i == 0)
    def _cast():
        for c in fetches_of(j):
            c.wait()
        cast()

    @pl.when((i == 1) & (j + 1 < pl.num_programs(0)))
    def _next_fetch():
        for c in fetches_of(j + 1):
            c.start()


def _gate_kernel(yp_ref, oa_ref, wp_ref, wa_ref, gp_ref, ga_ref, o_ref, wpb_ref, wab_ref, *stream,
                 layer, tn):
    if stream:
        wpf_ref, waf_ref, sem = stream

        def fetches_of(jj):
            cols = pl.ds(jj * tn, tn)
            return (pltpu.make_async_copy(wp_ref.at[layer, :, cols], wpf_ref, sem.at[0]),
                    pltpu.make_async_copy(wa_ref.at[layer, :, cols], waf_ref, sem.at[1]))

        def cast():
            wpb_ref[...] = wpf_ref[...].astype(BF16)
            wab_ref[...] = waf_ref[...].astype(BF16)

        _panel_stream(fetches_of, cast)
    else:
        @pl.when(pl.program_id(1) == 0)
        def _cast():
            wpb_ref[...] = wp_ref[...].astype(BF16)
            wab_ref[...] = wa_ref[...].astype(BF16)

    bp = jnp.dot(yp_ref[...], wpb_ref[...], preferred_element_type=F32)
    ba = jnp.dot(oa_ref[...], wab_ref[...], preferred_element_type=F32)
    merged = jax.nn.sigmoid(gp_ref[...]) * bp + jax.nn.sigmoid(ga_ref[...]) * ba
    o_ref[...] = merged.astype(o_ref.dtype)


def _gate_merge(y_pool, o_att, h, w_br_pool, w_br_att, layer, *, tm, tn):
    m = h.shape[1]
    gp0 = GATE_OFF // tn
    ga0 = (GATE_OFF + D_MODEL) // tn
    stream = m // tm > 1
    scratch = [pltpu.VMEM((POOL_W, tn), BF16), pltpu.VMEM((ATT_W, tn), BF16)]
    if stream:
        w_specs = [pl.BlockSpec(memory_space=pl.ANY)] * 2
        scratch += [pltpu.VMEM((POOL_W, tn), F32), pltpu.VMEM((ATT_W, tn), F32),
                    pltpu.SemaphoreType.DMA((2,))]
    else:
        w_specs = [pl.BlockSpec((None, POOL_W, tn), lambda j, i: (layer, 0, j)),
                   pl.BlockSpec((None, ATT_W, tn), lambda j, i: (layer, 0, j))]
    return pl.pallas_call(
        functools.partial(_gate_kernel, layer=layer, tn=tn),
        grid=(D_MODEL // tn, m // tm),
        in_specs=[pl.BlockSpec((tm, POOL_W), lambda j, i: (i, 0)),
                  pl.BlockSpec((tm, ATT_W), lambda j, i: (i, 0))] + w_specs + [
                  pl.BlockSpec((None, tm, tn), lambda j, i: (layer, i, gp0 + j)),
                  pl.BlockSpec((None, tm, tn), lambda j, i: (layer, i, ga0 + j))],
        out_specs=pl.BlockSpec((tm, tn), lambda j, i: (i, j)),
        out_shape=jax.ShapeDtypeStruct((m, D_MODEL), BF16),
        scratch_shapes=scratch,
        compiler_params=_params(2),
        name="gate_merge",
    )(y_pool, o_att, w_br_pool, w_br_att, h, h)


def _gelu_gate(a2, a1, a, b, cw_ref, cb_ref):
    ac = cw_ref[0:1] * a2 + cw_ref[1:2] * a1 + cw_ref[2:3] * a + cb_ref[...]
    gelu = 0.5 * ac * (1.0 + lax.erf(ac * (0.5 ** 0.5)))
    return gelu * b


def _ffn_up_seq_kernel(x_ref, w_ref, cw_ref, cb_ref, hm_ref, tail_ref, wq_ref,
                       wab_ref, wbb_ref, waf_ref, wbf_ref, carry_ref, sem, fsem,
                       *, layer, tm, tn, tiles_per_seq, n_chunks):
    j, i = pl.program_id(0), pl.program_id(1)
    n_panels = pl.num_programs(0)

    def panel_copies(jj):
        return (pltpu.make_async_copy(wab_ref, wq_ref.at[:, pl.ds(jj * tn, tn)], sem.at[0]),
                pltpu.make_async_copy(wbb_ref, wq_ref.at[:, pl.ds((n_panels + jj) * tn, tn)], sem.at[1]))

    def fetches_of(jj):
        return (pltpu.make_async_copy(w_ref.at[layer, :, pl.ds(jj * tn, tn)], waf_ref, fsem.at[0]),
                pltpu.make_async_copy(w_ref.at[layer, :, pl.ds((n_panels + jj) * tn, tn)], wbf_ref,
                                      fsem.at[1]))

    def cast():
        @pl.when(j > 0)
        def _drain():
            for c in panel_copies(j - 1):
                c.wait()
        wab_ref[...] = waf_ref[...].astype(BF16)
        wbb_ref[...] = wbf_ref[...].astype(BF16)
        for c in panel_copies(j):
            c.start()

    _panel_stream(fetches_of, cast)

    @pl.when(i % tiles_per_seq == 0)
    def _reset():
        carry_ref[...] = jnp.zeros_like(carry_ref)

    tc = tm // n_chunks
    row = lax.broadcasted_iota(jnp.int32, (tc, 1), 0)
    carry = carry_ref[...]
    for c in range(n_chunks):
        rows = pl.ds(c * tc, tc)
        x = x_ref[rows, :]
        a = jnp.dot(x, wab_ref[...], preferred_element_type=F32)
        b = jnp.dot(x, wbb_ref[...], preferred_element_type=F32)
        a1 = jnp.where(row == 0, carry[7:8], pltpu.roll(a, 1, 0))
        a2 = jnp.where(row == 0, carry[6:7], jnp.where(row == 1, carry[7:8], pltpu.roll(a, 2, 0)))
        hm_ref[rows, :] = _gelu_gate(a2, a1, a, b, cw_ref, cb_ref).astype(hm_ref.dtype)
        carry = a[tc - 8:]
    carry_ref[...] = carry
    tail_ref[0] = carry

    @pl.when((j == n_panels - 1) & (i == pl.num_programs(1) - 1))
    def _last_wait():
        for c in panel_copies(j):
            c.wait()


def _ffn_up_seq(x, w_up, conv_w, conv_b, layer, *, tm, tn, n_chunks):
    m, k = x.shape
    n_panels = D_FF // tn
    assert m // tm > 1
    return pl.pallas_call(
        functools.partial(_ffn_up_seq_kernel, layer=layer, tm=tm, tn=tn, tiles_per_seq=SEQ // tm,
                          n_chunks=n_chunks),
        grid=(n_panels, m // tm),
        in_specs=[pl.BlockSpec((tm, k), lambda j, i: (i, 0)),
                  pl.BlockSpec(memory_space=pl.ANY),
                  pl.BlockSpec((None, 3, tn), lambda j, i: (layer, 0, j)),
                  pl.BlockSpec((None, 1, tn), lambda j, i: (layer, 0, j))],
        out_specs=[pl.BlockSpec((tm, tn), lambda j, i: (i, j)),
                   pl.BlockSpec((1, 8, tn), lambda j, i: (i, 0, j)),
                   pl.BlockSpec(memory_space=pl.ANY)],
        out_shape=[jax.ShapeDtypeStruct((m, D_FF), BF16),
                   jax.ShapeDtypeStruct((m // tm, 8, D_FF), F32),
                   jax.ShapeDtypeStruct((k, 2 * D_FF), BF16)],
        scratch_shapes=[pltpu.VMEM((k, tn), BF16), pltpu.VMEM((k, tn), BF16),
                        pltpu.VMEM((k, tn), F32), pltpu.VMEM((k, tn), F32),
                        pltpu.VMEM((8, tn), F32), pltpu.SemaphoreType.DMA((2,)),
                        pltpu.SemaphoreType.DMA((2,))],
        compiler_params=_params(2),
        name="ffn_up",
    )(x, w_up, conv_w, conv_b)


def _ffn_up_dec_kernel(x_ref, wa_ref, wb_ref, cw_ref, cb_ref, p2_ref, p1_ref, hm_ref, a_ref):
    x = x_ref[...]
    a = jnp.dot(x, wa_ref[...], preferred_element_type=F32)
    b = jnp.dot(x, wb_ref[...], preferred_element_type=F32)
    a_ref[...] = a
    hm_ref[...] = _gelu_gate(p2_ref[...], p1_ref[...], a, b, cw_ref, cb_ref).astype(hm_ref.dtype)


def _ffn_up_dec(x, wq_up, conv_w, conv_b, layer, prev, *, tn):
    m, k = x.shape
    n_panels = D_FF // tn
    return pl.pallas_call(
        _ffn_up_dec_kernel,
        grid=(n_panels,),
        in_specs=[pl.BlockSpec((m, k), lambda j: (0, 0)),
                  pl.BlockSpec((k, tn), lambda j: (0, j)),
                  pl.BlockSpec((k, tn), lambda j: (0, n_panels + j)),
                  pl.BlockSpec((None, 3, tn), lambda j: (layer, 0, j)),
                  pl.BlockSpec((None, 1, tn), lambda j: (layer, 0, j)),
                  pl.BlockSpec((None, m, tn), lambda j: (0, 0, j)),
                  pl.BlockSpec((None, m, tn), lambda j: (1, 0, j))],
        out_specs=[pl.BlockSpec((m, tn), lambda j: (0, j)), pl.BlockSpec((m, tn), lambda j: (0, j))],
        out_shape=[jax.ShapeDtypeStruct((m, D_FF), BF16), jax.ShapeDtypeStruct((m, D_FF), F32)],
        compiler_params=_params(1),
        name="ffn_up_dec",
    )(x, wq_up, wq_up, conv_w, conv_b, prev, prev)


def _layer(resid, xb, h_prev, layer, p, *, dec=None):
    m = xb.shape[0]
    t = _tiles(m)
    out = {}
    if dec is None:
        tm_in, tn_in = t['proj']
        h, out['wq_in'] = _proj(xb, p['w_in'], layer, n=IN_W, tm=tm_in, tn=tn_in,
                                stack=(DEPTH, h_prev), emit_bf16=True, name="in_proj")
    else:
        h = _proj_q(xb, dec['wq_in'][layer], tn=t['proj'][1], layer=layer, stack=(DEPTH, h_prev),
                    name="in_proj_dec")
    if dec is None:
        y_pool = _pool_seq(h, p['w_pool_grp'], p['pool_scale'], layer, tm=t['pool'])
        o_att = _attn_seq(h, layer)
    else:
        y_pool = _pool_dec(h, dec['pool'], p['w_pool_grp'], p['pool_scale'], layer)
        h4 = h.reshape(DEPTH, m, IN_W // HEAD_DIM, HEAD_DIM)
        o = _attn_dec(h4, dec['caches'], layer).reshape(DEC_BATCH, ATT_W)
        o_att = jnp.pad(o, ((0, m - DEC_BATCH), (0, 0))).astype(BF16)
    merged = _gate_merge(y_pool, o_att, h, p['w_br_pool'], p['w_br_att'], layer,
                         tm=t['gate'][0], tn=t['gate'][1])
    z = _proj(merged, p['w_out'], layer, n=D_MODEL, tm=t['proj_first'][0], tn=t['proj_first'][1],
              resid=resid, alpha=ALPHA, name="out_proj")
    x1b, mu1, rstd1 = _layer_norm(z, p['ln1_g'], p['ln1_b'], layer, tm=t['ln'])
    if dec is None:
        hm, tail, out['wq_up'] = _ffn_up_seq(x1b, p['w_up'], p['conv_w'], p['conv_b'], layer,
                                             tm=t['ffn_up'][0], tn=t['ffn_up'][1],
                                             n_chunks=t['ffn_chunks'])
    else:
        hm, tail = _ffn_up_dec(x1b, dec['wq_up'][layer], p['conv_w'], p['conv_b'], layer,
                               dec['conv'][layer], tn=t['ffn_up'][1])
    z2 = _down(hm, p['w_down'], layer, (z, mu1, rstd1, p['ln1_g'], p['ln1_b'], layer),
               tm=t['down'][0], tn=t['down'][1], tk=t['down'][2])
    if layer == DEPTH - 1:
        out['y'] = _layer_norm(z2, p['ln2_g'], p['ln2_b'], layer, tm=t['ln'], full=True)
    else:
        out['xb'], mu2, rstd2 = _layer_norm(z2, p['ln2_g'], p['ln2_b'], layer, tm=t['ln'])
        out['resid'] = (z2, mu2, rstd2, p['ln2_g'], p['ln2_b'], layer)
    out.update(h=h, tail=tail)
    return out


def _run_prompt(x_prompt, p):
    batch = x_prompt.shape[0]
    resid = x_prompt.reshape(batch * SEQ, D_MODEL)
    xb = resid.astype(BF16)
    h, tails, wq_in, wq_up = None, [], [], []
    for l in range(DEPTH):
        o = _layer(resid, xb, h, l, p)
        h, resid, xb = o['h'], o.get('resid'), o.get('xb')
        tails.append(o['tail'])
        wq_in.append(o['wq_in'])
        wq_up.append(o['wq_up'])
    x = o['y']
    h4 = h.reshape(DEPTH, batch, SEQ, IN_W)
    pool = h4[:, :, SEQ - POOL_CTX:, :POOL_W]
    kvs = []
    for g, (win, _) in enumerate(ATT_GROUPS):
        base = POOL_W + 3 * g * ATT_W + ATT_W
        keep = min(win, SEQ)
        kvs.append(h4[:, :, SEQ - keep:, base:base + 2 * ATT_W]
                   .reshape(DEPTH, batch, keep, 2, N_HEADS, HEAD_DIM))
    tm = _tiles(batch * SEQ)['ffn_up'][0]
    t5 = jnp.stack(tails).reshape(DEPTH, batch, SEQ // tm, 8, D_FF)
    conv = t5[:, :, SEQ // tm - 1, 6:8, :]
    return (x.reshape(batch, SEQ, D_MODEL), pool, *kvs, conv), wq_in, wq_up


def _run_sample(x_sample, state_pool, caches_in, state_conv, p, wq_in, wq_up):
    pad = DEC_ROWS - DEC_BATCH
    resid = jnp.pad(x_sample.reshape(DEC_BATCH, D_MODEL), ((0, pad), (0, 0)))
    dec = dict(
        wq_in=wq_in, wq_up=wq_up,
        pool=jnp.pad(jnp.transpose(state_pool, (0, 2, 1, 3)), ((0, 0), (0, 0), (0, pad), (0, 0))),
        conv=jnp.pad(jnp.transpose(state_conv, (0, 2, 1, 3)), ((0, 0), (0, 0), (0, pad), (0, 0))),
        caches=[c.reshape(DEPTH, DEC_BATCH, c.shape[2] // dil, dil, 2, N_HEADS, HEAD_DIM)
                for c, (_, dil) in zip(caches_in, ATT_GROUPS)],
    )
    xb = resid.astype(BF16)
    h, a_new = None, []
    for l in range(DEPTH):
        o = _layer(resid, xb, h, l, p, dec=dec)
        h, resid, xb = o['h'], o.get('resid'), o.get('xb')
        a_new.append(o['tail'][:DEC_BATCH])
    x = o['y']
    h = h[:, :DEC_BATCH]
    pool = jnp.concatenate([state_pool[:, :, 1:], h[:, :, None, :POOL_W]], axis=2)
    kvs = []
    for g, (win, _) in enumerate(ATT_GROUPS):
        base = POOL_W + 3 * g * ATT_W + ATT_W
        kv_new = h[:, :, base:base + 2 * ATT_W].reshape(DEPTH, DEC_BATCH, 1, 2, N_HEADS, HEAD_DIM)
        keep = min(win, PAST_LEN + 1)
        kvs.append(jnp.concatenate([caches_in[g], kv_new], axis=2)[:, :, caches_in[g].shape[2] + 1 - keep:])
    conv = jnp.concatenate([state_conv[:, :, 1:], jnp.stack(a_new)[:, :, None, :]], axis=2)
    return (x[:DEC_BATCH].reshape(DEC_BATCH, 1, D_MODEL), pool, *kvs, conv)


def kernel(x_prompt, x_sample, state_pool, cache_kv1, cache_kv2, cache_kv3, state_conv, w_in, w_pool_grp, pool_scale, w_br_pool, w_br_att, w_out, ln1_g, ln1_b, w_up, conv_w, conv_b, w_down, ln2_g, ln2_b):
    def row3(t):
        return t.reshape(DEPTH, 1, t.shape[-1])

    p = dict(w_in=w_in, w_pool_grp=w_pool_grp, pool_scale=row3(pool_scale), w_br_pool=w_br_pool,
             w_br_att=w_br_att, w_out=w_out, ln1_g=row3(ln1_g), ln1_b=row3(ln1_b), w_up=w_up,
             conv_w=conv_w, conv_b=row3(conv_b), w_down=w_down, ln2_g=row3(ln2_g), ln2_b=row3(ln2_b))
    (yp, pool_p, kv1_p, kv2_p, kv3_p, conv_p), wq_in, wq_up = _run_prompt(x_prompt, p)
    ys, pool_s, kv1_s, kv2_s, kv3_s, conv_s = _run_sample(
        x_sample, state_pool, (cache_kv1, cache_kv2, cache_kv3), state_conv, p, wq_in, wq_up)
    return (yp, ys, pool_p, pool_s, kv1_p, kv1_s, kv2_p, kv2_s, kv3_p, kv3_s, conv_p, conv_s)
```

```python
import functools

import jax
import jax.numpy as jnp
from jax import lax
from jax.experimental import pallas as pl
from jax.experimental.pallas import tpu as pltpu

D_MODEL = 4096
SEQ = 2048
DEPTH = 2
DEC_BATCH = 8
PAST_LEN = 16384
HEAD_DIM = 128
N_HEADS = 8
ATT_GROUPS = ((128, 1), (512, 4), (2048, 16))
N_GROUPS = len(ATT_GROUPS)
ATT_W = N_HEADS * HEAD_DIM
POOL_WINDOWS = (2, 4, 8, 16)
POOL_W = D_MODEL // 2
POOL_GC = POOL_W // len(POOL_WINDOWS)
POOL_CTX = max(POOL_WINDOWS) - 1
D_FF = 11008
GATE_OFF = POOL_W + 3 * N_GROUPS * ATT_W
IN_W = GATE_OFF + 2 * D_MODEL
ALPHA = (2.0 * DEPTH) ** 0.25
LN_EPS = 1e-5
NEG = -1e30
ATT_BLK = 128
SCALE = HEAD_DIM ** -0.5

LANES = 128
DEC_ROWS = 16
HALO = 16
VMEM_LIMIT = 64 * 1024 * 1024

F32 = jnp.float32
BF16 = jnp.bfloat16

assert all(PAST_LEN >= win and win % dil == 0 for win, dil in ATT_GROUPS)


def _tiles(m):
    big = m >= 1024
    return dict(
        proj=(1024 if big else m, 1024),
        proj_first=(512 if big else m, 1024),
        gate=(1024 if big else m, 1024),
        ffn_up=(2048 if big else m, 256),
        ffn_chunks=4,
        down=(2048, 1024, 1280) if big else (m, 1024, 2048),
        ln=512 if big else m,
        pool=1024,
    )


def _params(n_axes, flags=None):
    return pltpu.CompilerParams(dimension_semantics=("arbitrary",) * n_axes,
                                vmem_limit_bytes=VMEM_LIMIT, flags=flags)


def _resid_operands(resid, tm, tn, at):
    if not isinstance(resid, tuple):
        return [pl.BlockSpec((tm, tn), lambda *g: at(*g))], [resid]
    z, mu, rstd, g3, b3, ln_layer = resid
    specs = [pl.BlockSpec((tm, tn), lambda *g: at(*g)),
             pl.BlockSpec((tm, LANES), lambda *g: (at(*g)[0], 0)),
             pl.BlockSpec((tm, LANES), lambda *g: (at(*g)[0], 0)),
             pl.BlockSpec((None, 1, tn), lambda *g: (ln_layer, 0, at(*g)[1])),
             pl.BlockSpec((None, 1, tn), lambda *g: (ln_layer, 0, at(*g)[1]))]
    return specs, [z, mu, rstd, g3, b3]


def _resid_value(r_refs):
    if len(r_refs) == 1:
        return r_refs[0][...]
    z_ref, mu_ref, rstd_ref, g_ref, b_ref = r_refs
    mu, rstd = mu_ref[...], rstd_ref[...]
    parts = []
    for c in range(z_ref.shape[1] // LANES):
        cols = slice(c * LANES, (c + 1) * LANES)
        parts.append((z_ref[:, cols] - mu) * rstd * g_ref[:, cols] + b_ref[:, cols])
    return jnp.concatenate(parts, axis=1)


def _proj_kernel(*refs, alpha, n_resid, has_alias, emit_bf16, stream_w, layer, tn):
    refs = list(refs)
    x_ref, w_ref = refs[:2]
    r_refs = refs[2:2 + n_resid]
    n_in = 2 + n_resid + int(has_alias)
    o_ref = refs[n_in]
    wq_ref = refs[n_in + 1] if emit_bf16 else None
    scratch = refs[n_in + 1 + int(emit_bf16):]
    wb_ref = scratch[0]
    wf_ref, wf_sem = (scratch[1], scratch[2]) if stream_w else (None, None)
    sem = refs[-1] if emit_bf16 else None
    j, i = pl.program_id(0), pl.program_id(1)
    n_j = pl.num_programs(0)

    def panel_copy(jj):
        return pltpu.make_async_copy(wb_ref, wq_ref.at[:, pl.ds(jj * tn, tn)], sem)

    def panel_fetch(jj):
        return pltpu.make_async_copy(w_ref.at[layer, :, pl.ds(jj * tn, tn)], wf_ref, wf_sem)

    if stream_w:
        @pl.when((j == 0) & (i == 0))
        def _first_fetch():
            panel_fetch(0).start()

    @pl.when(i == 0)
    def _cast():
        if emit_bf16:
            @pl.when(j > 0)
            def _drain():
                panel_copy(j - 1).wait()
        if stream_w:
            panel_fetch(j).wait()
            wb_ref[...] = wf_ref[...].astype(BF16)
        else:
            wb_ref[...] = w_ref[...].astype(BF16)
        if emit_bf16:
            panel_copy(j).start()

    if stream_w:
        @pl.when((i == 1) & (j + 1 < n_j))
        def _next_fetch():
            panel_fetch(j + 1).start()

    acc = jnp.dot(x_ref[...], wb_ref[...], preferred_element_type=F32)
    if n_resid:
        acc = alpha * _resid_value(r_refs) + acc
    if len(o_ref.shape) == 3:
        o_ref[0] = acc
        o_ref[1:] = jnp.zeros((o_ref.shape[0] - 1,) + acc.shape, o_ref.dtype)
    else:
        o_ref[...] = acc.astype(o_ref.dtype)

    if emit_bf16:
        @pl.when((j == pl.num_programs(0) - 1) & (i == pl.num_programs(1) - 1))
        def _last_wait():
            panel_copy(j).wait()


def _proj(x, w3, layer, *, n, tm, tn, resid=None, alpha=1.0, stack=None, emit_bf16=False, name):
    m, k = x.shape
    grid = (n // tn, m // tm)
    stream_w = grid[1] > 1
    w_spec = (pl.BlockSpec(memory_space=pl.ANY) if stream_w
              else pl.BlockSpec((None, k, tn), lambda j, i: (layer, 0, j)))
    in_specs = [pl.BlockSpec((tm, k), lambda j, i: (i, 0)), w_spec]
    args = [x, w3]
    n_resid = 0
    if resid is not None:
        r_specs, r_args = _resid_operands(resid, tm, tn, lambda j, i: (i, j))
        in_specs += r_specs
        args += r_args
        n_resid = len(r_args)
    aliases = {}
    if stack is None:
        out_shapes = [jax.ShapeDtypeStruct((m, n), F32)]
        out_specs = [pl.BlockSpec((tm, tn), lambda j, i: (i, j))]
    else:
        depth, prev = stack
        out_shapes = [jax.ShapeDtypeStruct((depth, m, n), F32)]
        out_specs = [pl.BlockSpec((None, tm, tn), lambda j, i: (layer, i, j))]
        if prev is None:
            assert layer == 0
            out_specs = [pl.BlockSpec((depth, tm, tn), lambda j, i: (0, i, j))]
        else:
            in_specs.append(pl.BlockSpec(memory_space=pl.ANY))
            aliases = {len(args): 0}
            args.append(prev)
    scratch = [pltpu.VMEM((k, tn), BF16)]
    if stream_w:
        scratch += [pltpu.VMEM((k, tn), F32), pltpu.SemaphoreType.DMA(())]
    if emit_bf16:
        out_shapes.append(jax.ShapeDtypeStruct((k, n), BF16))
        out_specs.append(pl.BlockSpec(memory_space=pl.ANY))
        scratch.append(pltpu.SemaphoreType.DMA(()))
    outs = pl.pallas_call(
        functools.partial(_proj_kernel, alpha=alpha, n_resid=n_resid, has_alias=bool(aliases),
                          emit_bf16=emit_bf16, stream_w=stream_w, layer=layer, tn=tn),
        grid=grid,
        in_specs=in_specs,
        out_specs=out_specs,
        out_shape=out_shapes,
        scratch_shapes=scratch,
        input_output_aliases=aliases,
        compiler_params=_params(2),
        name=name,
    )(*args)
    return tuple(outs) if emit_bf16 else outs[0]


def _proj_q(x, wq, *, tn, layer, stack, name):
    m, k = x.shape
    n = wq.shape[1]
    depth, prev = stack
    in_specs = [pl.BlockSpec((m, k), lambda j: (0, 0)), pl.BlockSpec((k, tn), lambda j: (0, j))]
    args = [x, wq]
    aliases = {}
    if prev is None:
        assert layer == 0
        out_spec = pl.BlockSpec((depth, m, tn), lambda j: (0, 0, j))
        kern = _proj_q_first_kernel
    else:
        out_spec = pl.BlockSpec((None, m, tn), lambda j: (layer, 0, j))
        in_specs.append(pl.BlockSpec(memory_space=pl.ANY))
        aliases = {2: 0}
        args.append(prev)
        kern = _proj_q_next_kernel
    return pl.pallas_call(
        kern,
        grid=(n // tn,),
        in_specs=in_specs,
        out_specs=out_spec,
        out_shape=jax.ShapeDtypeStruct((depth, m, n), F32),
        input_output_aliases=aliases,
        compiler_params=_params(1),
        name=name,
    )(*args)


def _proj_q_first_kernel(x_ref, w_ref, o_ref):
    o_ref[0] = jnp.dot(x_ref[...], w_ref[...], preferred_element_type=F32)
    o_ref[1:] = jnp.zeros((o_ref.shape[0] - 1,) + o_ref.shape[1:], o_ref.dtype)


def _proj_q_next_kernel(x_ref, w_ref, prev_ref, o_ref):
    del prev_ref
    o_ref[...] = jnp.dot(x_ref[...], w_ref[...], preferred_element_type=F32)


def _down_kernel(*refs, alpha, rem):
    x_ref, w_ref = refs[:2]
    r_refs = refs[2:-1]
    o_ref = refs[-1]
    kk = pl.program_id(2)
    nk = pl.num_programs(2)

    @pl.when(kk == 0)
    def _first():
        o_ref[...] = alpha * _resid_value(r_refs) + jnp.dot(x_ref[...], w_ref[...].astype(BF16),
                                                            preferred_element_type=F32)

    @pl.when((kk > 0) & (kk < nk - 1))
    def _full():
        o_ref[...] += jnp.dot(x_ref[...], w_ref[...].astype(BF16), preferred_element_type=F32)

    @pl.when(kk == nk - 1)
    def _last():
        o_ref[...] += jnp.dot(x_ref[:, :rem], w_ref[:rem, :].astype(BF16), preferred_element_type=F32)


def _down(x, w3, layer, resid, *, tm, tn, tk):
    m, k = x.shape
    n = w3.shape[2]
    nk = pl.cdiv(k, tk)
    assert nk >= 2
    rem = k - (nk - 1) * tk
    r_specs, r_args = _resid_operands(resid, tm, tn, lambda j, i, kk: (i, j))
    return pl.pallas_call(
        functools.partial(_down_kernel, alpha=ALPHA, rem=rem),
        grid=(n // tn, m // tm, nk),
        in_specs=[pl.BlockSpec((tm, tk), lambda j, i, kk: (i, kk)),
                  pl.BlockSpec((None, tk, tn), lambda j, i, kk: (layer, kk, j))] + r_specs,
        out_specs=pl.BlockSpec((tm, tn), lambda j, i, kk: (i, j)),
        out_shape=jax.ShapeDtypeStruct((m, n), F32),
        compiler_params=_params(3),
        name="down_proj",
    )(x, w3, *r_args)


def _ln_kernel(z_ref, g_ref, b_ref, *o_refs, full):
    z = z_ref[...]
    mu = jnp.mean(z, axis=-1, keepdims=True)
    zc = z - mu
    var = jnp.mean(zc * zc, axis=-1, keepdims=True)
    rstd = lax.rsqrt(var + LN_EPS)
    y = zc * rstd * g_ref[...] + b_ref[...]
    if full:
        o_refs[0][...] = y
    else:
        ob_ref, mu_ref, rstd_ref = o_refs
        ob_ref[...] = y.astype(BF16)
        mu_ref[...] = jnp.broadcast_to(mu, mu_ref.shape)
        rstd_ref[...] = jnp.broadcast_to(rstd, rstd_ref.shape)


def _layer_norm(z, g3, b3, layer, *, tm, full=False):
    m, d = z.shape
    row = pl.BlockSpec((tm, d), lambda i: (i, 0))
    col = pl.BlockSpec((tm, LANES), lambda i: (i, 0))
    if full:
        out_specs, out_shape = [row], [jax.ShapeDtypeStruct((m, d), F32)]
    else:
        out_specs = [row, col, col]
        out_shape = [jax.ShapeDtypeStruct((m, d), BF16), jax.ShapeDtypeStruct((m, LANES), F32),
                     jax.ShapeDtypeStruct((m, LANES), F32)]
    outs = pl.pallas_call(
        functools.partial(_ln_kernel, full=full),
        grid=(m // tm,),
        in_specs=[row,
                  pl.BlockSpec((None, 1, d), lambda i: (layer, 0, 0)),
                  pl.BlockSpec((None, 1, d), lambda i: (layer, 0, 0))],
        out_specs=out_specs,
        out_shape=out_shape,
        compiler_params=_params(1),
        name="layer_norm",
    )(z, g3, b3)
    return outs[0] if full else tuple(outs)


def _pool_groups(window_sum_fn, u_fn, inv_cnt_fn, wb_ref, sc_ref, y_ref):
    for g, w in enumerate(POOL_WINDOWS):
        cols = slice(g * POOL_GC, (g + 1) * POOL_GC)
        d = window_sum_fn(g, w, cols) * inv_cnt_fn(w) - u_fn(cols)
        y = jnp.dot(d.astype(BF16), wb_ref[g], preferred_element_type=F32) * sc_ref[:, cols]
        y_ref[:, cols] = y.astype(y_ref.dtype)


def _pool_seq_kernel(u_ref, halo_ref, wg_ref, sc_ref, y_ref, wb_ref, *, tm, tiles_per_seq):
    i = pl.program_id(0)

    @pl.when(i == 0)
    def _cast():
        wb_ref[...] = wg_ref[...].astype(BF16)

    t_in_seq = i % tiles_per_seq
    halo = jnp.where(t_in_seq == 0, 0.0, halo_ref[...])
    pos = t_in_seq * tm + lax.broadcasted_iota(jnp.int32, (tm, 1), 0)

    def window_sum(g, w, cols):
        s = jnp.concatenate([halo[:, cols], u_ref[:, cols]], axis=0)
        k = 1
        while k < w:
            s = s + pltpu.roll(s, k, 0)
            k *= 2
        return s[HALO:]

    def inv_cnt(w):
        return 1.0 / jnp.minimum(pos + 1, w).astype(F32)

    _pool_groups(window_sum, lambda cols: u_ref[:, cols], inv_cnt, wb_ref, sc_ref, y_ref)


def _pool_seq(h, w_pool_grp, pool_scale, layer, *, tm):
    m = h.shape[1]
    tiles_per_seq = SEQ // tm
    return pl.pallas_call(
        functools.partial(_pool_seq_kernel, tm=tm, tiles_per_seq=tiles_per_seq),
        grid=(m // tm,),
        in_specs=[pl.BlockSpec((None, tm, POOL_W), lambda i: (layer, i, 0)),
                  pl.BlockSpec((None, HALO, POOL_W),
                               lambda i: (layer, jnp.maximum(i * (tm // HALO) - 1, 0), 0)),
                  pl.BlockSpec((None, len(POOL_WINDOWS), POOL_GC, POOL_GC), lambda i: (layer, 0, 0, 0)),
                  pl.BlockSpec((None, 1, POOL_W), lambda i: (layer, 0, 0))],
        out_specs=pl.BlockSpec((tm, POOL_W), lambda i: (i, 0)),
        out_shape=jax.ShapeDtypeStruct((m, POOL_W), BF16),
        scratch_shapes=[pltpu.VMEM((len(POOL_WINDOWS), POOL_GC, POOL_GC), BF16)],
        compiler_params=_params(1),
        name="pool_seq",
    )(h, h, w_pool_grp, pool_scale)


def _pool_dec_kernel(u_ref, st_ref, wg_ref, sc_ref, y_ref, wb_ref):
    wb_ref[...] = wg_ref[...].astype(BF16)

    def window_sum(g, w, cols):
        s = u_ref[:, cols]
        for k in range(1, w):
            s = s + st_ref[POOL_CTX - k, :, cols]
        return s

    def inv_cnt(w):
        return 1.0 / float(min(PAST_LEN + 1, w))

    _pool_groups(window_sum, lambda cols: u_ref[:, cols], inv_cnt, wb_ref, sc_ref, y_ref)


def _pool_dec(h, st, w_pool_grp, pool_scale, layer):
    m = h.shape[1]
    return pl.pallas_call(
        _pool_dec_kernel,
        grid=(1,),
        in_specs=[pl.BlockSpec((None, m, POOL_W), lambda i: (layer, 0, 0)),
                  pl.BlockSpec((None, POOL_CTX, m, POOL_W), lambda i: (layer, 0, 0, 0)),
                  pl.BlockSpec((None, len(POOL_WINDOWS), POOL_GC, POOL_GC), lambda i: (layer, 0, 0, 0)),
                  pl.BlockSpec((None, 1, POOL_W), lambda i: (layer, 0, 0))],
        out_specs=pl.BlockSpec((m, POOL_W), lambda i: (0, 0)),
        out_shape=jax.ShapeDtypeStruct((m, POOL_W), BF16),
        scratch_shapes=[pltpu.VMEM((len(POOL_WINDOWS), POOL_GC, POOL_GC), BF16)],
        compiler_params=_params(1),
        name="pool_dec",
    )(h, st, w_pool_grp, pool_scale)


def _merge_groups(o_list, lse_list):
    mx = functools.reduce(jnp.maximum, lse_list)
    ws = [jnp.exp(l - mx) for l in lse_list]
    num = functools.reduce(lambda a, b: a + b, [w * o for w, o in zip(ws, o_list)])
    den = functools.reduce(lambda a, b: a + b, ws)
    return num / den


def _attn_seq_kernel(*refs):
    qkv = refs[:9]
    o_ref = refs[9]
    og_refs = refs[10:13]
    lg_refs = refs[13:16]

    qi = lax.broadcasted_iota(jnp.int32, (ATT_BLK, ATT_BLK), 0)
    kj = lax.broadcasted_iota(jnp.int32, (ATT_BLK, ATT_BLK), 1)
    cur_mask = kj <= qi
    qi2 = lax.broadcasted_iota(jnp.int32, (ATT_BLK, 2 * ATT_BLK), 0)
    kj2 = lax.broadcasted_iota(jnp.int32, (ATT_BLK, 2 * ATT_BLK), 1)
    both_mask = (kj2 >= qi2) & (kj2 - qi2 <= ATT_BLK)

    for g, (win, dil) in enumerate(ATT_GROUPS):
        q_ref, k_ref, v_ref = qkv[3 * g:3 * g + 3]
        n_blk = SEQ // dil // ATT_BLK
        for r in range(dil):
            for nb in range(n_blk):
                q_rows = pl.ds(r + nb * ATT_BLK * dil, ATT_BLK, stride=dil)
                if nb == 0:
                    kv_rows, mask = q_rows, cur_mask
                else:
                    kv_rows = pl.ds(r + (nb - 1) * ATT_BLK * dil, 2 * ATT_BLK, stride=dil)
                    mask = both_mask
                q = q_ref[q_rows, :].astype(BF16)
                k = k_ref[kv_rows, :].astype(BF16)
                v = v_ref[kv_rows, :].astype(BF16)
                s = lax.dot_general(q, k, (((1,), (1,)), ((), ())), preferred_element_type=F32) * SCALE
                s = jnp.where(mask, s, NEG)
                mx = jnp.max(s, axis=-1, keepdims=True)
                p = jnp.exp(s - mx)
                den = jnp.sum(p, axis=-1, keepdims=True)
                o = jnp.dot(p.astype(BF16), v, preferred_element_type=F32) / den
                og_refs[g][q_rows, :] = o
                lg_refs[g][q_rows, :] = jnp.broadcast_to(mx + jnp.log(den), (ATT_BLK, HEAD_DIM))

    chunk = 256
    for c in range(SEQ // chunk):
        rows = pl.ds(c * chunk, chunk)
        out = _merge_groups([r_[rows, :] for r_ in og_refs], [r_[rows, :] for r_ in lg_refs])
        o_ref[rows, :] = out.astype(o_ref.dtype)


def _attn_seq(h, layer):
    m = h.shape[1]
    in_specs = []
    for g in range(N_GROUPS):
        for which in range(3):
            blk0 = (POOL_W + (3 * g + which) * ATT_W) // HEAD_DIM
            in_specs.append(pl.BlockSpec((None, SEQ, HEAD_DIM),
                                         lambda b, hd, blk0=blk0: (layer, b, blk0 + hd)))
    return pl.pallas_call(
        _attn_seq_kernel,
        grid=(m // SEQ, N_HEADS),
        in_specs=in_specs,
        out_specs=pl.BlockSpec((SEQ, HEAD_DIM), lambda b, hd: (b, hd)),
        out_shape=jax.ShapeDtypeStruct((m, ATT_W), BF16),
        scratch_shapes=[pltpu.VMEM((SEQ, HEAD_DIM), F32)] * (2 * N_GROUPS),
        compiler_params=_params(2),
        name="attn_seq",
    )(*([h] * 9))


def _attn_dec_kernel(h_ref, c1_ref, c2_ref, c3_ref, o_ref):
    def rnd(t):
        return t.astype(BF16).astype(F32)

    outs, lses = [], []
    for g, c_ref in enumerate((c1_ref, c2_ref, c3_ref)):
        r0 = (POOL_W + 3 * g * ATT_W) // HEAD_DIM
        q = rnd(h_ref[r0:r0 + N_HEADS, :])
        k_new = rnd(h_ref[r0 + N_HEADS:r0 + 2 * N_HEADS, :])
        v_new = rnd(h_ref[r0 + 2 * N_HEADS:r0 + 3 * N_HEADS, :])
        kc = rnd(c_ref[:, 0])
        vc = rnd(c_ref[:, 1])
        s = jnp.sum(kc * q[None], axis=-1, keepdims=True) * SCALE
        s_new = jnp.sum(k_new * q, axis=-1, keepdims=True) * SCALE
        mx = jnp.maximum(jnp.max(s, axis=0), s_new)
        p = jnp.exp(s - mx[None])
        p_new = jnp.exp(s_new - mx)
        den = jnp.sum(p, axis=0) + p_new
        acc = jnp.sum(rnd(p) * vc, axis=0) + rnd(p_new) * v_new
        outs.append(acc / den)
        lses.append(mx + jnp.log(den))
    o_ref[...] = _merge_groups(outs, lses)


def _attn_dec(h4, caches, layer):
    in_specs = [pl.BlockSpec((None, None, IN_W // HEAD_DIM, HEAD_DIM), lambda b: (layer, b, 0, 0))]
    for c in caches:
        in_specs.append(pl.BlockSpec((None, None, c.shape[2], None, 2, N_HEADS, HEAD_DIM),
                                     lambda b: (layer, b, 0, 0, 0, 0, 0)))
    return pl.pallas_call(
        _attn_dec_kernel,
        grid=(DEC_BATCH,),
        in_specs=in_specs,
        out_specs=pl.BlockSpec((None, N_HEADS, HEAD_DIM), lambda b: (b, 0, 0)),
        out_shape=jax.ShapeDtypeStruct((DEC_BATCH, N_HEADS, HEAD_DIM), F32),
        compiler_params=_params(1),
        name="attn_dec",
    )(h4, *caches)


def _panel_stream(fetches_of, cast):
    j, i = pl.program_id(0), pl.program_id(1)

    @pl.when((j == 0) & (i == 0))
    def _first_fetch():
        for c in fetches_of(0):
            c.start()

    @pl.when(i == 0)
    def _cast():
        for c in fetches_of(j):
            c.wait()
        cast()

    @pl.when((i == 1) & (j + 1 < pl.num_programs(0)))
    def _next_fetch():
        for c in fetches_of(j + 1):
            c.start()


def _gate_kernel(yp_ref, oa_ref, wp_ref, wa_ref, gp_ref, ga_ref, o_ref, wpb_ref, wab_ref, *stream,
                 layer, tn):
    if stream:
        wpf_ref, waf_ref, sem = stream

        def fetches_of(jj):
            cols = pl.ds(jj * tn, tn)
            return (pltpu.make_async_copy(wp_ref.at[layer, :, cols], wpf_ref, sem.at[0]),
                    pltpu.make_async_copy(wa_ref.at[layer, :, cols], waf_ref, sem.at[1]))

        def cast():
            wpb_ref[...] = wpf_ref[...].astype(BF16)
            wab_ref[...] = waf_ref[...].astype(BF16)

        _panel_stream(fetches_of, cast)
    else:
        @pl.when(pl.program_id(1) == 0)
        def _cast():
            wpb_ref[...] = wp_ref[...].astype(BF16)
            wab_ref[...] = wa_ref[...].astype(BF16)

    bp = jnp.dot(yp_ref[...], wpb_ref[...], preferred_element_type=F32)
    ba = jnp.dot(oa_ref[...], wab_ref[...], preferred_element_type=F32)
    merged = jax.nn.sigmoid(gp_ref[...]) * bp + jax.nn.sigmoid(ga_ref[...]) * ba
    o_ref[...] = merged.astype(o_ref.dtype)


def _gate_merge(y_pool, o_att, h, w_br_pool, w_br_att, layer, *, tm, tn):
    m = h.shape[1]
    gp0 = GATE_OFF // tn
    ga0 = (GATE_OFF + D_MODEL) // tn
    stream = m // tm > 1
    scratch = [pltpu.VMEM((POOL_W, tn), BF16), pltpu.VMEM((ATT_W, tn), BF16)]
    if stream:
        w_specs = [pl.BlockSpec(memory_space=pl.ANY)] * 2
        scratch += [pltpu.VMEM((POOL_W, tn), F32), pltpu.VMEM((ATT_W, tn), F32),
                    pltpu.SemaphoreType.DMA((2,))]
    else:
        w_specs = [pl.BlockSpec((None, POOL_W, tn), lambda j, i: (layer, 0, j)),
                   pl.BlockSpec((None, ATT_W, tn), lambda j, i: (layer, 0, j))]
    return pl.pallas_call(
        functools.partial(_gate_kernel, layer=layer, tn=tn),
        grid=(D_MODEL // tn, m // tm),
        in_specs=[pl.BlockSpec((tm, POOL_W), lambda j, i: (i, 0)),
                  pl.BlockSpec((tm, ATT_W), lambda j, i: (i, 0))] + w_specs + [
                  pl.BlockSpec((None, tm, tn), lambda j, i: (layer, i, gp0 + j)),
                  pl.BlockSpec((None, tm, tn), lambda j, i: (layer, i, ga0 + j))],
        out_specs=pl.BlockSpec((tm, tn), lambda j, i: (i, j)),
        out_shape=jax.ShapeDtypeStruct((m, D_MODEL), BF16),
        scratch_shapes=scratch,
        compiler_params=_params(2),
        name="gate_merge",
    )(y_pool, o_att, w_br_pool, w_br_att, h, h)


def _gelu_gate(a2, a1, a, b, cw_ref, cb_ref):
    ac = cw_ref[0:1] * a2 + cw_ref[1:2] * a1 + cw_ref[2:3] * a + cb_ref[...]
    gelu = 0.5 * ac * (1.0 + lax.erf(ac * (0.5 ** 0.5)))
    return gelu * b


def _ffn_up_seq_kernel(x_ref, w_ref, cw_ref, cb_ref, hm_ref, tail_ref, wq_ref,
                       wab_ref, wbb_ref, waf_ref, wbf_ref, carry_ref, sem, fsem,
                       *, layer, tm, tn, tiles_per_seq, n_chunks):
    j, i = pl.program_id(0), pl.program_id(1)
    n_panels = pl.num_programs(0)

    def panel_copies(jj):
        return (pltpu.make_async_copy(wab_ref, wq_ref.at[:, pl.ds(jj * tn, tn)], sem.at[0]),
                pltpu.make_async_copy(wbb_ref, wq_ref.at[:, pl.ds((n_panels + jj) * tn, tn)], sem.at[1]))

    def fetches_of(jj):
        return (pltpu.make_async_copy(w_ref.at[layer, :, pl.ds(jj * tn, tn)], waf_ref, fsem.at[0]),
                pltpu.make_async_copy(w_ref.at[layer, :, pl.ds((n_panels + jj) * tn, tn)], wbf_ref,
                                      fsem.at[1]))

    def cast():
        @pl.when(j > 0)
        def _drain():
            for c in panel_copies(j - 1):
                c.wait()
        wab_ref[...] = waf_ref[...].astype(BF16)
        wbb_ref[...] = wbf_ref[...].astype(BF16)
        for c in panel_copies(j):
            c.start()

    _panel_stream(fetches_of, cast)

    @pl.when(i % tiles_per_seq == 0)
    def _reset():
        carry_ref[...] = jnp.zeros_like(carry_ref)

    tc = tm // n_chunks
    row = lax.broadcasted_iota(jnp.int32, (tc, 1), 0)
    carry = carry_ref[...]
    for c in range(n_chunks):
        rows = pl.ds(c * tc, tc)
        x = x_ref[rows, :]
        a = jnp.dot(x, wab_ref[...], preferred_element_type=F32)
        b = jnp.dot(x, wbb_ref[...], preferred_element_type=F32)
        a1 = jnp.where(row == 0, carry[7:8], pltpu.roll(a, 1, 0))
        a2 = jnp.where(row == 0, carry[6:7], jnp.where(row == 1, carry[7:8], pltpu.roll(a, 2, 0)))
        hm_ref[rows, :] = _gelu_gate(a2, a1, a, b, cw_ref, cb_ref).astype(hm_ref.dtype)
        carry = a[tc - 8:]
    carry_ref[...] = carry
    tail_ref[0] = carry

    @pl.when((j == n_panels - 1) & (i == pl.num_programs(1) - 1))
    def _last_wait():
        for c in panel_copies(j):
            c.wait()


def _ffn_up_seq(x, w_up, conv_w, conv_b, layer, *, tm, tn, n_chunks):
    m, k = x.shape
    n_panels = D_FF // tn
    assert m // tm > 1
    return pl.pallas_call(
        functools.partial(_ffn_up_seq_kernel, layer=layer, tm=tm, tn=tn, tiles_per_seq=SEQ // tm,
                          n_chunks=n_chunks),
        grid=(n_panels, m // tm),
        in_specs=[pl.BlockSpec((tm, k), lambda j, i: (i, 0)),
                  pl.BlockSpec(memory_space=pl.ANY),
                  pl.BlockSpec((None, 3, tn), lambda j, i: (layer, 0, j)),
                  pl.BlockSpec((None, 1, tn), lambda j, i: (layer, 0, j))],
        out_specs=[pl.BlockSpec((tm, tn), lambda j, i: (i, j)),
                   pl.BlockSpec((1, 8, tn), lambda j, i: (i, 0, j)),
                   pl.BlockSpec(memory_space=pl.ANY)],
        out_shape=[jax.ShapeDtypeStruct((m, D_FF), BF16),
                   jax.ShapeDtypeStruct((m // tm, 8, D_FF), F32),
                   jax.ShapeDtypeStruct((k, 2 * D_FF), BF16)],
        scratch_shapes=[pltpu.VMEM((k, tn), BF16), pltpu.VMEM((k, tn), BF16),
                        pltpu.VMEM((k, tn), F32), pltpu.VMEM((k, tn), F32),
                        pltpu.VMEM((8, tn), F32), pltpu.SemaphoreType.DMA((2,)),
                        pltpu.SemaphoreType.DMA((2,))],
        compiler_params=_params(2),
        name="ffn_up",
    )(x, w_up, conv_w, conv_b)


def _ffn_up_dec_kernel(x_ref, wa_ref, wb_ref, cw_ref, cb_ref, p2_ref, p1_ref, hm_ref, a_ref):
    x = x_ref[...]
    a = jnp.dot(x, wa_ref[...], preferred_element_type=F32)
    b = jnp.dot(x, wb_ref[...], preferred_element_type=F32)
    a_ref[...] = a
    hm_ref[...] = _gelu_gate(p2_ref[...], p1_ref[...], a, b, cw_ref, cb_ref).astype(hm_ref.dtype)


def _ffn_up_dec(x, wq_up, conv_w, conv_b, layer, prev, *, tn):
    m, k = x.shape
    n_panels = D_FF // tn
    return pl.pallas_call(
        _ffn_up_dec_kernel,
        grid=(n_panels,),
        in_specs=[pl.BlockSpec((m, k), lambda j: (0, 0)),
                  pl.BlockSpec((k, tn), lambda j: (0, j)),
                  pl.BlockSpec((k, tn), lambda j: (0, n_panels + j)),
                  pl.BlockSpec((None, 3, tn), lambda j: (layer, 0, j)),
                  pl.BlockSpec((None, 1, tn), lambda j: (layer, 0, j)),
                  pl.BlockSpec((None, m, tn), lambda j: (0, 0, j)),
                  pl.BlockSpec((None, m, tn), lambda j: (1, 0, j))],
        out_specs=[pl.BlockSpec((m, tn), lambda j: (0, j)), pl.BlockSpec((m, tn), lambda j: (0, j))],
        out_shape=[jax.ShapeDtypeStruct((m, D_FF), BF16), jax.ShapeDtypeStruct((m, D_FF), F32)],
        compiler_params=_params(1),
        name="ffn_up_dec",
    )(x, wq_up, wq_up, conv_w, conv_b, prev, prev)


def _layer(resid, xb, h_prev, layer, p, *, dec=None):
    m = xb.shape[0]
    t = _tiles(m)
    out = {}
    if dec is None:
        tm_in, tn_in = t['proj']
        h, out['wq_in'] = _proj(xb, p['w_in'], layer, n=IN_W, tm=tm_in, tn=tn_in,
                                stack=(DEPTH, h_prev), emit_bf16=True, name="in_proj")
    else:
        h = _proj_q(xb, dec['wq_in'][layer], tn=t['proj'][1], layer=layer, stack=(DEPTH, h_prev),
                    name="in_proj_dec")
    if dec is None:
        y_pool = _pool_seq(h, p['w_pool_grp'], p['pool_scale'], layer, tm=t['pool'])
        o_att = _attn_seq(h, layer)
    else:
        y_pool = _pool_dec(h, dec['pool'], p['w_pool_grp'], p['pool_scale'], layer)
        h4 = h.reshape(DEPTH, m, IN_W // HEAD_DIM, HEAD_DIM)
        o = _attn_dec(h4, dec['caches'], layer).reshape(DEC_BATCH, ATT_W)
        o_att = jnp.pad(o, ((0, m - DEC_BATCH), (0, 0))).astype(BF16)
    merged = _gate_merge(y_pool, o_att, h, p['w_br_pool'], p['w_br_att'], layer,
                         tm=t['gate'][0], tn=t['gate'][1])
    z = _proj(merged, p['w_out'], layer, n=D_MODEL, tm=t['proj_first'][0], tn=t['proj_first'][1],
              resid=resid, alpha=ALPHA, name="out_proj")
    x1b, mu1, rstd1 = _layer_norm(z, p['ln1_g'], p['ln1_b'], layer, tm=t['ln'])
    if dec is None:
        hm, tail, out['wq_up'] = _ffn_up_seq(x1b, p['w_up'], p['conv_w'], p['conv_b'], layer,
                                             tm=t['ffn_up'][0], tn=t['ffn_up'][1],
                                             n_chunks=t['ffn_chunks'])
    else:
        hm, tail = _ffn_up_dec(x1b, dec['wq_up'][layer], p['conv_w'], p['conv_b'], layer,
                               dec['conv'][layer], tn=t['ffn_up'][1])
    z2 = _down(hm, p['w_down'], layer, (z, mu1, rstd1, p['ln1_g'], p['ln1_b'], layer),
               tm=t['down'][0], tn=t['down'][1], tk=t['down'][2])
    if layer == DEPTH - 1:
        out['y'] = _layer_norm(z2, p['ln2_g'], p['ln2_b'], layer, tm=t['ln'], full=True)
    else:
        out['xb'], mu2, rstd2 = _layer_norm(z2, p['ln2_g'], p['ln2_b'], layer, tm=t['ln'])
        out['resid'] = (z2, mu2, rstd2, p['ln2_g'], p['ln2_b'], layer)
    out.update(h=h, tail=tail)
    return out


def _run_prompt(x_prompt, p):
    batch = x_prompt.shape[0]
    resid = x_prompt.reshape(batch * SEQ, D_MODEL)
    xb = resid.astype(BF16)
    h, tails, wq_in, wq_up = None, [], [], []
    for l in range(DEPTH):
        o = _layer(resid, xb, h, l, p)
        h, resid, xb = o['h'], o.get('resid'), o.get('xb')
        tails.append(o['tail'])
        wq_in.append(o['wq_in'])
        wq_up.append(o['wq_up'])
    x = o['y']
    h4 = h.reshape(DEPTH, batch, SEQ, IN_W)
    pool = h4[:, :, SEQ - POOL_CTX:, :POOL_W]
    kvs = []
    for g, (win, _) in enumerate(ATT_GROUPS):
        base = POOL_W + 3 * g * ATT_W + ATT_W
        keep = min(win, SEQ)
        kvs.append(h4[:, :, SEQ - keep:, base:base + 2 * ATT_W]
                   .reshape(DEPTH, batch, keep, 2, N_HEADS, HEAD_DIM))
    tm = _tiles(batch * SEQ)['ffn_up'][0]
    t5 = jnp.stack(tails).reshape(DEPTH, batch, SEQ // tm, 8, D_FF)
    conv = t5[:, :, SEQ // tm - 1, 6:8, :]
    return (x.reshape(batch, SEQ, D_MODEL), pool, *kvs, conv), wq_in, wq_up


def _run_sample(x_sample, state_pool, caches_in, state_conv, p, wq_in, wq_up):
    pad = DEC_ROWS - DEC_BATCH
    resid = jnp.pad(x_sample.reshape(DEC_BATCH, D_MODEL), ((0, pad), (0, 0)))
    dec = dict(
        wq_in=wq_in, wq_up=wq_up,
        pool=jnp.pad(jnp.transpose(state_pool, (0, 2, 1, 3)), ((0, 0), (0, 0), (0, pad), (0, 0))),
        conv=jnp.pad(jnp.transpose(state_conv, (0, 2, 1, 3)), ((0, 0), (0, 0), (0, pad), (0, 0))),
        caches=[c.reshape(DEPTH, DEC_BATCH, c.shape[2] // dil, dil, 2, N_HEADS, HEAD_DIM)
                for c, (_, dil) in zip(caches_in, ATT_GROUPS)],
    )
    xb = resid.astype(BF16)
    h, a_new = None, []
    for l in range(DEPTH):
        o = _layer(resid, xb, h, l, p, dec=dec)
        h, resid, xb = o['h'], o.get('resid'), o.get('xb')
        a_new.append(o['tail'][:DEC_BATCH])
    x = o['y']
    h = h[:, :DEC_BATCH]
    pool = jnp.concatenate([state_pool[:, :, 1:], h[:, :, None, :POOL_W]], axis=2)
    kvs = []
    for g, (win, _) in enumerate(ATT_GROUPS):
        base = POOL_W + 3 * g * ATT_W + ATT_W
        kv_new = h[:, :, base:base + 2 * ATT_W].reshape(DEPTH, DEC_BATCH, 1, 2, N_HEADS, HEAD_DIM)
        keep = min(win, PAST_LEN + 1)
        kvs.append(jnp.concatenate([caches_in[g], kv_new], axis=2)[:, :, caches_in[g].shape[2] + 1 - keep:])
    conv = jnp.concatenate([state_conv[:, :, 1:], jnp.stack(a_new)[:, :, None, :]], axis=2)
    return (x[:DEC_BATCH].reshape(DEC_BATCH, 1, D_MODEL), pool, *kvs, conv)


def kernel(x_prompt, x_sample, state_pool, cache_kv1, cache_kv2, cache_kv3, state_conv, w_in, w_pool_grp, pool_scale, w_br_pool, w_br_att, w_out, ln1_g, ln1_b, w_up, conv_w, conv_b, w_down, ln2_g, ln2_b):
    def row3(t):
        return t.reshape(DEPTH, 1, t.shape[-1])

    p = dict(w_in=w_in, w_pool_grp=w_pool_grp, pool_scale=row3(pool_scale), w_br_pool=w_br_pool,
             w_br_att=w_br_att, w_out=w_out, ln1_g=row3(ln1_g), ln1_b=row3(ln1_b), w_up=w_up,
             conv_w=conv_w, conv_b=row3(conv_b), w_down=w_down, ln2_g=row3(ln2_g), ln2_b=row3(ln2_b))
    (yp, pool_p, kv1_p, kv2_p, kv3_p, conv_p), wq_in, wq_up = _run_prompt(x_prompt, p)
    ys, pool_s, kv1_s, kv2_s, kv3_s, conv_s = _run_sample(
        x_sample, state_pool, (cache_kv1, cache_kv2, cache_kv3), state_conv, p, wq_in, wq_up)
    return (yp, ys, pool_p, pool_s, kv1_p, kv1_s, kv2_p, kv2_s, kv3_p, kv3_s, conv_p, conv_s)
```

```python
import functools

import jax
import jax.numpy as jnp
from jax import lax
from jax.experimental import pallas as pl
from jax.experimental.pallas import tpu as pltpu

D_MODEL = 4096
SEQ = 2048
DEPTH = 2
DEC_BATCH = 8
PAST_LEN = 16384
HEAD_DIM = 128
N_HEADS = 8
ATT_GROUPS = ((128, 1), (512, 4), (2048, 16))
N_GROUPS = len(ATT_GROUPS)
ATT_W = N_HEADS * HEAD_DIM
POOL_WINDOWS = (2, 4, 8, 16)
POOL_W = D_MODEL // 2
POOL_GC = POOL_W // len(POOL_WINDOWS)
POOL_CTX = max(POOL_WINDOWS) - 1
D_FF = 11008
GATE_OFF = POOL_W + 3 * N_GROUPS * ATT_W
IN_W = GATE_OFF + 2 * D_MODEL
ALPHA = (2.0 * DEPTH) ** 0.25
LN_EPS = 1e-5
NEG = -1e30
ATT_BLK = 128
SCALE = HEAD_DIM ** -0.5

LANES = 128
SUBLANES = 8
DEC_ROWS = 16
HALO = 16
VMEM_LIMIT = 64 * 1024 * 1024

F32 = jnp.float32
BF16 = jnp.bfloat16

assert all(PAST_LEN >= win and win % dil == 0 for win, dil in ATT_GROUPS)


def _tiles(m):
    big = m >= 1024
    return dict(
        proj=(1024 if big else m, 1024),
        proj_first=(512 if big else m, 1024),
        gate=(1024 if big else m, 1024),
        ffn_up=(2048 if big else m, 256),
        ffn_chunks=4,
        down=(2048, 1024, 1024) if big else (m, 1024, 2048),
        ln=512 if big else m,
        pool=512,
    )


def _params(n_axes, flags=None):
    return pltpu.CompilerParams(dimension_semantics=("arbitrary",) * n_axes,
                                vmem_limit_bytes=VMEM_LIMIT, flags=flags)


def _panel_stream(fetches_of, cast):
    j, i = pl.program_id(0), pl.program_id(1)

    @pl.when((j == 0) & (i == 0))
    def _first_fetch():
        for c in fetches_of(0):
            c.start()

    @pl.when(i == 0)
    def _cast():
        for c in fetches_of(j):
            c.wait()
        cast()

    @pl.when((i == 1) & (j + 1 < pl.num_programs(0)))
    def _next_fetch():
        for c in fetches_of(j + 1):
            c.start()


def _resid_operands(resid, tm, tn, at):
    if not isinstance(resid, tuple):
        return [pl.BlockSpec((tm, tn), lambda *g: at(*g))], [resid]
    z, mu, rstd, g3, b3, ln_layer = resid
    specs = [pl.BlockSpec((tm, tn), lambda *g: at(*g)),
             pl.BlockSpec((tm, LANES), lambda *g: (at(*g)[0], 0)),
             pl.BlockSpec((tm, LANES), lambda *g: (at(*g)[0], 0)),
             pl.BlockSpec((None, 1, tn), lambda *g: (ln_layer, 0, at(*g)[1])),
             pl.BlockSpec((None, 1, tn), lambda *g: (ln_layer, 0, at(*g)[1]))]
    return specs, [z, mu, rstd, g3, b3]


def _resid_value(r_refs):
    if len(r_refs) == 1:
        return r_refs[0][...]
    z_ref, mu_ref, rstd_ref, g_ref, b_ref = r_refs
    mu, rstd = mu_ref[...], rstd_ref[...]
    parts = []
    for c in range(z_ref.shape[1] // LANES):
        cols = slice(c * LANES, (c + 1) * LANES)
        parts.append((z_ref[:, cols] - mu) * rstd * g_ref[:, cols] + b_ref[:, cols])
    return jnp.concatenate(parts, axis=1)


def _proj_kernel(*refs, alpha, n_resid, has_alias, emit_bf16, stream_w, layer, tn):
    refs = list(refs)
    x_ref, w_ref = refs[:2]
    r_refs = refs[2:2 + n_resid]
    n_in = 2 + n_resid + int(has_alias)
    o_ref = refs[n_in]
    wq_ref = refs[n_in + 1] if emit_bf16 else None
    scratch = refs[n_in + 1 + int(emit_bf16):]
    wb_ref = scratch[0]
    wf_ref, wf_sem = (scratch[1], scratch[2]) if stream_w else (None, None)
    sem = refs[-1] if emit_bf16 else None
    j, i = pl.program_id(0), pl.program_id(1)

    def panel_copy(jj):
        return pltpu.make_async_copy(wb_ref, wq_ref.at[:, pl.ds(jj * tn, tn)], sem)

    def fetches_of(jj):
        return (pltpu.make_async_copy(w_ref.at[layer, :, pl.ds(jj * tn, tn)], wf_ref, wf_sem),)

    def cast_from(src_ref):
        def cast():
            if emit_bf16:
                @pl.when(j > 0)
                def _drain():
                    panel_copy(j - 1).wait()
            wb_ref[...] = src_ref[...].astype(BF16)
            if emit_bf16:
                panel_copy(j).start()
        return cast

    if stream_w:
        _panel_stream(fetches_of, cast_from(wf_ref))
    else:
        pl.when(i == 0)(cast_from(w_ref))

    acc = jnp.dot(x_ref[...], wb_ref[...], preferred_element_type=F32)
    if n_resid:
        acc = alpha * _resid_value(r_refs) + acc
    if len(o_ref.shape) == 3:
        o_ref[0] = acc
        o_ref[1:] = jnp.zeros((o_ref.shape[0] - 1,) + acc.shape, o_ref.dtype)
    else:
        o_ref[...] = acc.astype(o_ref.dtype)

    if emit_bf16:
        @pl.when((j == pl.num_programs(0) - 1) & (i == pl.num_programs(1) - 1))
        def _last_wait():
            panel_copy(j).wait()


def _proj(x, w3, layer, *, n, tm, tn, resid=None, alpha=1.0, stack=None, emit_bf16=False, name):
    m, k = x.shape
    grid = (n // tn, m // tm)
    stream_w = grid[1] > 1
    w_spec = (pl.BlockSpec(memory_space=pl.ANY) if stream_w
              else pl.BlockSpec((None, k, tn), lambda j, i: (layer, 0, j)))
    in_specs = [pl.BlockSpec((tm, k), lambda j, i: (i, 0)), w_spec]
    args = [x, w3]
    n_resid = 0
    if resid is not None:
        r_specs, r_args = _resid_operands(resid, tm, tn, lambda j, i: (i, j))
        in_specs += r_specs
        args += r_args
        n_resid = len(r_args)
    aliases = {}
    if stack is None:
        out_shapes = [jax.ShapeDtypeStruct((m, n), F32)]
        out_specs = [pl.BlockSpec((tm, tn), lambda j, i: (i, j))]
    else:
        depth, prev = stack
        out_shapes = [jax.ShapeDtypeStruct((depth, m, n), F32)]
        out_specs = [pl.BlockSpec((None, tm, tn), lambda j, i: (layer, i, j))]
        if prev is None:
            assert layer == 0
            out_specs = [pl.BlockSpec((depth, tm, tn), lambda j, i: (0, i, j))]
        else:
            in_specs.append(pl.BlockSpec(memory_space=pl.ANY))
            aliases = {len(args): 0}
            args.append(prev)
    scratch = [pltpu.VMEM((k, tn), BF16)]
    if stream_w:
        scratch += [pltpu.VMEM((k, tn), F32), pltpu.SemaphoreType.DMA(())]
    if emit_bf16:
        out_shapes.append(jax.ShapeDtypeStruct((k, n), BF16))
        out_specs.append(pl.BlockSpec(memory_space=pl.ANY))
        scratch.append(pltpu.SemaphoreType.DMA(()))
    outs = pl.pallas_call(
        functools.partial(_proj_kernel, alpha=alpha, n_resid=n_resid, has_alias=bool(aliases),
                          emit_bf16=emit_bf16, stream_w=stream_w, layer=layer, tn=tn),
        grid=grid,
        in_specs=in_specs,
        out_specs=out_specs,
        out_shape=out_shapes,
        scratch_shapes=scratch,
        input_output_aliases=aliases,
        compiler_params=_params(2),
        name=name,
    )(*args)
    return tuple(outs) if emit_bf16 else outs[0]


def _proj_q(x, wq, *, tn, layer, stack, name):
    m, k = x.shape
    n = wq.shape[1]
    depth, prev = stack
    in_specs = [pl.BlockSpec((m, k), lambda j: (0, 0)), pl.BlockSpec((k, tn), lambda j: (0, j))]
    args = [x, wq]
    aliases = {}
    if prev is None:
        assert layer == 0
        out_spec = pl.BlockSpec((depth, m, tn), lambda j: (0, 0, j))
        kern = _proj_q_first_kernel
    else:
        out_spec = pl.BlockSpec((None, m, tn), lambda j: (layer, 0, j))
        in_specs.append(pl.BlockSpec(memory_space=pl.ANY))
        aliases = {2: 0}
        args.append(prev)
        kern = _proj_q_next_kernel
    return pl.pallas_call(
        kern,
        grid=(n // tn,),
        in_specs=in_specs,
        out_specs=out_spec,
        out_shape=jax.ShapeDtypeStruct((depth, m, n), F32),
        input_output_aliases=aliases,
        compiler_params=_params(1),
        name=name,
    )(*args)


def _proj_q_first_kernel(x_ref, w_ref, o_ref):
    o_ref[0] = jnp.dot(x_ref[...], w_ref[...], preferred_element_type=F32)
    o_ref[1:] = jnp.zeros((o_ref.shape[0] - 1,) + o_ref.shape[1:], o_ref.dtype)


def _proj_q_next_kernel(x_ref, w_ref, prev_ref, o_ref):
    del prev_ref
    o_ref[...] = jnp.dot(x_ref[...], w_ref[...], preferred_element_type=F32)


def _down_kernel(*refs, alpha, rem):
    x_ref, w_ref = refs[:2]
    r_refs = refs[2:-1]
    o_ref = refs[-1]
    kk = pl.program_id(2)
    nk = pl.num_programs(2)

    @pl.when(kk == 0)
    def _first():
        o_ref[...] = alpha * _resid_value(r_refs) + jnp.dot(x_ref[...], w_ref[...].astype(BF16),
                                                            preferred_element_type=F32)

    @pl.when((kk > 0) & (kk < nk - 1))
    def _full():
        o_ref[...] += jnp.dot(x_ref[...], w_ref[...].astype(BF16), preferred_element_type=F32)

    @pl.when(kk == nk - 1)
    def _last():
        o_ref[...] += jnp.dot(x_ref[:, :rem], w_ref[:rem, :].astype(BF16), preferred_element_type=F32)


def _down(x, w3, layer, resid, *, tm, tn, tk):
    m, k = x.shape
    n = w3.shape[2]
    nk = pl.cdiv(k, tk)
    assert nk >= 2
    rem = k - (nk - 1) * tk
    r_specs, r_args = _resid_operands(resid, tm, tn, lambda j, i, kk: (i, j))
    return pl.pallas_call(
        functools.partial(_down_kernel, alpha=ALPHA, rem=rem),
        grid=(n // tn, m // tm, nk),
        in_specs=[pl.BlockSpec((tm, tk), lambda j, i, kk: (i, kk)),
                  pl.BlockSpec((None, tk, tn), lambda j, i, kk: (layer, kk, j))] + r_specs,
        out_specs=pl.BlockSpec((tm, tn), lambda j, i, kk: (i, j)),
        out_shape=jax.ShapeDtypeStruct((m, n), F32),
        compiler_params=_params(3),
        name="down_proj",
    )(x, w3, *r_args)


def _ln_kernel(z_ref, g_ref, b_ref, *o_refs, full):
    z = z_ref[...]
    mu = jnp.mean(z, axis=-1, keepdims=True)
    zc = z - mu
    var = jnp.mean(zc * zc, axis=-1, keepdims=True)
    rstd = lax.rsqrt(var + LN_EPS)
    y = zc * rstd * g_ref[...] + b_ref[...]
    if full:
        o_refs[0][...] = y
    else:
        ob_ref, mu_ref, rstd_ref = o_refs
        ob_ref[...] = y.astype(BF16)
        mu_ref[...] = jnp.broadcast_to(mu, mu_ref.shape)
        rstd_ref[...] = jnp.broadcast_to(rstd, rstd_ref.shape)


def _layer_norm(z, g3, b3, layer, *, tm, full=False):
    m, d = z.shape
    row = pl.BlockSpec((tm, d), lambda i: (i, 0))
    col = pl.BlockSpec((tm, LANES), lambda i: (i, 0))
    if full:
        out_specs, out_shape = [row], [jax.ShapeDtypeStruct((m, d), F32)]
    else:
        out_specs = [row, col, col]
        out_shape = [jax.ShapeDtypeStruct((m, d), BF16), jax.ShapeDtypeStruct((m, LANES), F32),
                     jax.ShapeDtypeStruct((m, LANES), F32)]
    outs = pl.pallas_call(
        functools.partial(_ln_kernel, full=full),
        grid=(m // tm,),
        in_specs=[row,
                  pl.BlockSpec((None, 1, d), lambda i: (layer, 0, 0)),
                  pl.BlockSpec((None, 1, d), lambda i: (layer, 0, 0))],
        out_specs=out_specs,
        out_shape=out_shape,
        compiler_params=_params(1),
        name="layer_norm",
    )(z, g3, b3)
    return outs[0] if full else tuple(outs)


def _pool_groups(window_sum_fn, u_fn, inv_cnt_fn, wb_ref, sc_ref, y_ref):
    for g, w in enumerate(POOL_WINDOWS):
        cols = slice(g * POOL_GC, (g + 1) * POOL_GC)
        d = window_sum_fn(g, w, cols) * inv_cnt_fn(w) - u_fn(cols)
        y = jnp.dot(d.astype(BF16), wb_ref[g], preferred_element_type=F32) * sc_ref[:, cols]
        y_ref[:, cols] = y.astype(y_ref.dtype)


def _pool_seq_kernel(u_ref, halo_ref, wg_ref, sc_ref, y_ref, wb_ref, *, tm, tiles_per_seq):
    i = pl.program_id(0)

    @pl.when(i == 0)
    def _cast():
        wb_ref[...] = wg_ref[...].astype(BF16)

    t_in_seq = i % tiles_per_seq
    halo = jnp.where(t_in_seq == 0, 0.0, halo_ref[...])
    pos = t_in_seq * tm + lax.broadcasted_iota(jnp.int32, (tm, 1), 0)

    def window_sum(g, w, cols):
        s = jnp.concatenate([halo[:, cols], u_ref[:, cols]], axis=0)
        k = 1
        while k < w:
            s = s + pltpu.roll(s, k, 0)
            k *= 2
        return s[HALO:]

    def inv_cnt(w):
        return 1.0 / jnp.minimum(pos + 1, w).astype(F32)

    _pool_groups(window_sum, lambda cols: u_ref[:, cols], inv_cnt, wb_ref, sc_ref, y_ref)


def _pool_seq(h, w_pool_grp, pool_scale, layer, *, tm):
    m = h.shape[1]
    tiles_per_seq = SEQ // tm
    return pl.pallas_call(
        functools.partial(_pool_seq_kernel, tm=tm, tiles_per_seq=tiles_per_seq),
        grid=(m // tm,),
        in_specs=[pl.BlockSpec((None, tm, POOL_W), lambda i: (layer, i, 0)),
                  pl.BlockSpec((None, HALO, POOL_W),
                               lambda i: (layer, jnp.maximum(i * (tm // HALO) - 1, 0), 0)),
                  pl.BlockSpec((None, len(POOL_WINDOWS), POOL_GC, POOL_GC), lambda i: (layer, 0, 0, 0)),
                  pl.BlockSpec((None, 1, POOL_W), lambda i: (layer, 0, 0))],
        out_specs=pl.BlockSpec((tm, POOL_W), lambda i: (i, 0)),
        out_shape=jax.ShapeDtypeStruct((m, POOL_W), BF16),
        scratch_shapes=[pltpu.VMEM((len(POOL_WINDOWS), POOL_GC, POOL_GC), BF16)],
        compiler_params=_params(1),
        name="pool_seq",
    )(h, h, w_pool_grp, pool_scale)


def _pool_dec_kernel(u_ref, st_ref, wg_ref, sc_ref, y_ref, wb_ref):
    wb_ref[...] = wg_ref[...].astype(BF16)

    def window_sum(g, w, cols):
        s = u_ref[:, cols]
        for k in range(1, w):
            s = s + st_ref[POOL_CTX - k, :, cols]
        return s

    def inv_cnt(w):
        return 1.0 / float(min(PAST_LEN + 1, w))

    _pool_groups(window_sum, lambda cols: u_ref[:, cols], inv_cnt, wb_ref, sc_ref, y_ref)


def _pool_dec(h, st, w_pool_grp, pool_scale, layer):
    m = h.shape[1]
    return pl.pallas_call(
        _pool_dec_kernel,
        grid=(1,),
        in_specs=[pl.BlockSpec((None, m, POOL_W), lambda i: (layer, 0, 0)),
                  pl.BlockSpec((None, POOL_CTX, m, POOL_W), lambda i: (layer, 0, 0, 0)),
                  pl.BlockSpec((None, len(POOL_WINDOWS), POOL_GC, POOL_GC), lambda i: (layer, 0, 0, 0)),
                  pl.BlockSpec((None, 1, POOL_W), lambda i: (layer, 0, 0))],
        out_specs=pl.BlockSpec((m, POOL_W), lambda i: (0, 0)),
        out_shape=jax.ShapeDtypeStruct((m, POOL_W), BF16),
        scratch_shapes=[pltpu.VMEM((len(POOL_WINDOWS), POOL_GC, POOL_GC), BF16)],
        compiler_params=_params(1),
        name="pool_dec",
    )(h, st, w_pool_grp, pool_scale)


def _merge_groups(o_list, lse_list):
    mx = functools.reduce(jnp.maximum, lse_list)
    ws = [jnp.exp(l - mx) for l in lse_list]
    num = functools.reduce(lambda a, b: a + b, [w * o for w, o in zip(ws, o_list)])
    den = functools.reduce(lambda a, b: a + b, ws)
    return num / den


def _attn_seq_kernel(*refs):
    qkv = refs[:9]
    o_ref = refs[9]
    og_refs = refs[10:13]
    lg_refs = refs[13:16]

    qi = lax.broadcasted_iota(jnp.int32, (ATT_BLK, ATT_BLK), 0)
    kj = lax.broadcasted_iota(jnp.int32, (ATT_BLK, ATT_BLK), 1)
    cur_mask = kj <= qi
    qi2 = lax.broadcasted_iota(jnp.int32, (ATT_BLK, 2 * ATT_BLK), 0)
    kj2 = lax.broadcasted_iota(jnp.int32, (ATT_BLK, 2 * ATT_BLK), 1)
    both_mask = (kj2 >= qi2) & (kj2 - qi2 <= ATT_BLK)

    for g, (win, dil) in enumerate(ATT_GROUPS):
        q_ref, k_ref, v_ref = qkv[3 * g:3 * g + 3]
        n_blk = SEQ // dil // ATT_BLK
        for r in range(dil):
            for nb in range(n_blk):
                q_rows = pl.ds(r + nb * ATT_BLK * dil, ATT_BLK, stride=dil)
                if nb == 0:
                    kv_rows, mask = q_rows, cur_mask
                else:
                    kv_rows = pl.ds(r + (nb - 1) * ATT_BLK * dil, 2 * ATT_BLK, stride=dil)
                    mask = both_mask
                q = q_ref[q_rows, :].astype(BF16)
                k = k_ref[kv_rows, :].astype(BF16)
                v = v_ref[kv_rows, :].astype(BF16)
                s = lax.dot_general(q, k, (((1,), (1,)), ((), ())), preferred_element_type=F32) * SCALE
                s = jnp.where(mask, s, NEG)
                mx = jnp.max(s, axis=-1, keepdims=True)
                p = jnp.exp(s - mx)
                den = jnp.sum(p, axis=-1, keepdims=True)
                o = jnp.dot(p.astype(BF16), v, preferred_element_type=F32) / den
                og_refs[g][q_rows, :] = o
                lg_refs[g][q_rows, :] = jnp.broadcast_to(mx + jnp.log(den), (ATT_BLK, HEAD_DIM))

    chunk = 256
    for c in range(SEQ // chunk):
        rows = pl.ds(c * chunk, chunk)
        out = _merge_groups([r_[rows, :] for r_ in og_refs], [r_[rows, :] for r_ in lg_refs])
        o_ref[rows, :] = out.astype(o_ref.dtype)


def _attn_seq(h, layer):
    m = h.shape[1]
    in_specs = []
    for g in range(N_GROUPS):
        for which in range(3):
            blk0 = (POOL_W + (3 * g + which) * ATT_W) // HEAD_DIM
            in_specs.append(pl.BlockSpec((None, SEQ, HEAD_DIM),
                                         lambda b, hd, blk0=blk0: (layer, b, blk0 + hd)))
    return pl.pallas_call(
        _attn_seq_kernel,
        grid=(m // SEQ, N_HEADS),
        in_specs=in_specs,
        out_specs=pl.BlockSpec((SEQ, HEAD_DIM), lambda b, hd: (b, hd)),
        out_shape=jax.ShapeDtypeStruct((m, ATT_W), BF16),
        scratch_shapes=[pltpu.VMEM((SEQ, HEAD_DIM), F32)] * (2 * N_GROUPS),
        compiler_params=_params(2),
        name="attn_seq",
    )(*([h] * 9))


def _attn_dec_kernel(h_ref, c1_ref, c2_ref, c3_ref, o_ref):
    def rnd(t):
        return t.astype(BF16).astype(F32)

    outs, lses = [], []
    for g, c_ref in enumerate((c1_ref, c2_ref, c3_ref)):
        r0 = (POOL_W + 3 * g * ATT_W) // HEAD_DIM
        q = rnd(h_ref[r0:r0 + N_HEADS, :])
        k_new = rnd(h_ref[r0 + N_HEADS:r0 + 2 * N_HEADS, :])
        v_new = rnd(h_ref[r0 + 2 * N_HEADS:r0 + 3 * N_HEADS, :])
        kc = rnd(c_ref[:, 0])
        vc = rnd(c_ref[:, 1])
        s = jnp.sum(kc * q[None], axis=-1, keepdims=True) * SCALE
        s_new = jnp.sum(k_new * q, axis=-1, keepdims=True) * SCALE
        mx = jnp.maximum(jnp.max(s, axis=0), s_new)
        p = jnp.exp(s - mx[None])
        p_new = jnp.exp(s_new - mx)
        den = jnp.sum(p, axis=0) + p_new
        acc = jnp.sum(rnd(p) * vc, axis=0) + rnd(p_new) * v_new
        outs.append(acc / den)
        lses.append(mx + jnp.log(den))
    o_ref[...] = _merge_groups(outs, lses)


def _attn_dec(h4, caches, layer):
    in_specs = [pl.BlockSpec((None, None, IN_W // HEAD_DIM, HEAD_DIM), lambda b: (layer, b, 0, 0))]
    for c in caches:
        in_specs.append(pl.BlockSpec((None, None, c.shape[2], None, 2, N_HEADS, HEAD_DIM),
                                     lambda b: (layer, b, 0, 0, 0, 0, 0)))
    return pl.pallas_call(
        _attn_dec_kernel,
        grid=(DEC_BATCH,),
        in_specs=in_specs,
        out_specs=pl.BlockSpec((None, N_HEADS, HEAD_DIM), lambda b: (b, 0, 0)),
        out_shape=jax.ShapeDtypeStruct((DEC_BATCH, N_HEADS, HEAD_DIM), F32),
        compiler_params=_params(1),
        name="attn_dec",
    )(h4, *caches)


def _gate_kernel(yp_ref, oa_ref, wp_ref, wa_ref, gp_ref, ga_ref, o_ref, wpb_ref, wab_ref, *stream,
                 layer, tn):
    if stream:
        wpf_ref, waf_ref, sem = stream

        def fetches_of(jj):
            cols = pl.ds(jj * tn, tn)
            return (pltpu.make_async_copy(wp_ref.at[layer, :, cols], wpf_ref, sem.at[0]),
                    pltpu.make_async_copy(wa_ref.at[layer, :, cols], waf_ref, sem.at[1]))

        def cast():
            wpb_ref[...] = wpf_ref[...].astype(BF16)
            wab_ref[...] = waf_ref[...].astype(BF16)

        _panel_stream(fetches_of, cast)
    else:
        @pl.when(pl.program_id(1) == 0)
        def _cast():
            wpb_ref[...] = wp_ref[...].astype(BF16)
            wab_ref[...] = wa_ref[...].astype(BF16)

    bp = jnp.dot(yp_ref[...], wpb_ref[...], preferred_element_type=F32)
    ba = jnp.dot(oa_ref[...], wab_ref[...], preferred_element_type=F32)
    merged = jax.nn.sigmoid(gp_ref[...]) * bp + jax.nn.sigmoid(ga_ref[...]) * ba
    o_ref[...] = merged.astype(o_ref.dtype)


def _gate_merge(y_pool, o_att, h, w_br_pool, w_br_att, layer, *, tm, tn):
    m = h.shape[1]
    gp0 = GATE_OFF // tn
    ga0 = (GATE_OFF + D_MODEL) // tn
    stream = m // tm > 1
    scratch = [pltpu.VMEM((POOL_W, tn), BF16), pltpu.VMEM((ATT_W, tn), BF16)]
    if stream:
        w_specs = [pl.BlockSpec(memory_space=pl.ANY)] * 2
        scratch += [pltpu.VMEM((POOL_W, tn), F32), pltpu.VMEM((ATT_W, tn), F32),
                    pltpu.SemaphoreType.DMA((2,))]
    else:
        w_specs = [pl.BlockSpec((None, POOL_W, tn), lambda j, i: (layer, 0, j)),
                   pl.BlockSpec((None, ATT_W, tn), lambda j, i: (layer, 0, j))]
    return pl.pallas_call(
        functools.partial(_gate_kernel, layer=layer, tn=tn),
        grid=(D_MODEL // tn, m // tm),
        in_specs=[pl.BlockSpec((tm, POOL_W), lambda j, i: (i, 0)),
                  pl.BlockSpec((tm, ATT_W), lambda j, i: (i, 0))] + w_specs + [
                  pl.BlockSpec((None, tm, tn), lambda j, i: (layer, i, gp0 + j)),
                  pl.BlockSpec((None, tm, tn), lambda j, i: (layer, i, ga0 + j))],
        out_specs=pl.BlockSpec((tm, tn), lambda j, i: (i, j)),
        out_shape=jax.ShapeDtypeStruct((m, D_MODEL), BF16),
        scratch_shapes=scratch,
        compiler_params=_params(2),
        name="gate_merge",
    )(y_pool, o_att, w_br_pool, w_br_att, h, h)


def _gelu_gate(a2, a1, a, b, cw_ref, cb_ref):
    ac = cw_ref[0:1] * a2 + cw_ref[1:2] * a1 + cw_ref[2:3] * a + cb_ref[...]
    gelu = 0.5 * ac * (1.0 + lax.erf(ac * (0.5 ** 0.5)))
    return gelu * b


def _ffn_up_seq_kernel(x_ref, w_ref, cw_ref, cb_ref, hm_ref, tail_ref, wq_ref,
                       wab_ref, wbb_ref, waf_ref, wbf_ref, carry_ref, sem, fsem,
                       *, layer, tm, tn, tiles_per_seq, n_chunks):
    j, i = pl.program_id(0), pl.program_id(1)
    n_panels = pl.num_programs(0)

    def panel_copies(jj):
        return (pltpu.make_async_copy(wab_ref, wq_ref.at[:, pl.ds(jj * tn, tn)], sem.at[0]),
                pltpu.make_async_copy(wbb_ref, wq_ref.at[:, pl.ds((n_panels + jj) * tn, tn)], sem.at[1]))

    def fetches_of(jj):
        return (pltpu.make_async_copy(w_ref.at[layer, :, pl.ds(jj * tn, tn)], waf_ref, fsem.at[0]),
                pltpu.make_async_copy(w_ref.at[layer, :, pl.ds((n_panels + jj) * tn, tn)], wbf_ref,
                                      fsem.at[1]))

    def cast():
        @pl.when(j > 0)
        def _drain():
            for c in panel_copies(j - 1):
                c.wait()
        wab_ref[...] = waf_ref[...].astype(BF16)
        wbb_ref[...] = wbf_ref[...].astype(BF16)
        for c in panel_copies(j):
            c.start()

    _panel_stream(fetches_of, cast)

    @pl.when(i % tiles_per_seq == 0)
    def _reset():
        carry_ref[...] = jnp.zeros_like(carry_ref)

    tc = tm // n_chunks
    row = lax.broadcasted_iota(jnp.int32, (tc, 1), 0)
    carry = carry_ref[...]
    for c in range(n_chunks):
        rows = pl.ds(c * tc, tc)
        x = x_ref[rows, :]
        a = jnp.dot(x, wab_ref[...], preferred_element_type=F32)
        b = jnp.dot(x, wbb_ref[...], preferred_element_type=F32)
        prev1, prev2 = carry[SUBLANES - 1:SUBLANES], carry[SUBLANES - 2:SUBLANES - 1]
        a1 = jnp.where(row == 0, prev1, pltpu.roll(a, 1, 0))
        a2 = jnp.where(row == 0, prev2, jnp.where(row == 1, prev1, pltpu.roll(a, 2, 0)))
        hm_ref[rows, :] = _gelu_gate(a2, a1, a, b, cw_ref, cb_ref).astype(hm_ref.dtype)
        carry = a[tc - SUBLANES:]
    carry_ref[...] = carry
    tail_ref[0] = carry

    @pl.when((j == n_panels - 1) & (i == pl.num_programs(1) - 1))
    def _last_wait():
        for c in panel_copies(j):
            c.wait()


def _ffn_up_seq(x, w_up, conv_w, conv_b, layer, *, tm, tn, n_chunks):
    m, k = x.shape
    n_panels = D_FF // tn
    assert m // tm > 1
    return pl.pallas_call(
        functools.partial(_ffn_up_seq_kernel, layer=layer, tm=tm, tn=tn, tiles_per_seq=SEQ // tm,
                          n_chunks=n_chunks),
        grid=(n_panels, m // tm),
        in_specs=[pl.BlockSpec((tm, k), lambda j, i: (i, 0)),
                  pl.BlockSpec(memory_space=pl.ANY),
                  pl.BlockSpec((None, 3, tn), lambda j, i: (layer, 0, j)),
                  pl.BlockSpec((None, 1, tn), lambda j, i: (layer, 0, j))],
        out_specs=[pl.BlockSpec((tm, tn), lambda j, i: (i, j)),
                   pl.BlockSpec((1, SUBLANES, tn), lambda j, i: (i, 0, j)),
                   pl.BlockSpec(memory_space=pl.ANY)],
        out_shape=[jax.ShapeDtypeStruct((m, D_FF), BF16),
                   jax.ShapeDtypeStruct((m // tm, SUBLANES, D_FF), F32),
                   jax.ShapeDtypeStruct((k, 2 * D_FF), BF16)],
        scratch_shapes=[pltpu.VMEM((k, tn), BF16), pltpu.VMEM((k, tn), BF16),
                        pltpu.VMEM((k, tn), F32), pltpu.VMEM((k, tn), F32),
                        pltpu.VMEM((SUBLANES, tn), F32), pltpu.SemaphoreType.DMA((2,)),
                        pltpu.SemaphoreType.DMA((2,))],
        compiler_params=_params(2),
        name="ffn_up",
    )(x, w_up, conv_w, conv_b)


def _ffn_up_dec_kernel(x_ref, wa_ref, wb_ref, cw_ref, cb_ref, p2_ref, p1_ref, hm_ref, a_ref):
    x = x_ref[...]
    a = jnp.dot(x, wa_ref[...], preferred_element_type=F32)
    b = jnp.dot(x, wb_ref[...], preferred_element_type=F32)
    a_ref[...] = a
    hm_ref[...] = _gelu_gate(p2_ref[...], p1_ref[...], a, b, cw_ref, cb_ref).astype(hm_ref.dtype)


def _ffn_up_dec(x, wq_up, conv_w, conv_b, layer, prev, *, tn):
    m, k = x.shape
    n_panels = D_FF // tn
    return pl.pallas_call(
        _ffn_up_dec_kernel,
        grid=(n_panels,),
        in_specs=[pl.BlockSpec((m, k), lambda j: (0, 0)),
                  pl.BlockSpec((k, tn), lambda j: (0, j)),
                  pl.BlockSpec((k, tn), lambda j: (0, n_panels + j)),
                  pl.BlockSpec((None, 3, tn), lambda j: (layer, 0, j)),
                  pl.BlockSpec((None, 1, tn), lambda j: (layer, 0, j)),
                  pl.BlockSpec((None, m, tn), lambda j: (0, 0, j)),
                  pl.BlockSpec((None, m, tn), lambda j: (1, 0, j))],
        out_specs=[pl.BlockSpec((m, tn), lambda j: (0, j)), pl.BlockSpec((m, tn), lambda j: (0, j))],
        out_shape=[jax.ShapeDtypeStruct((m, D_FF), BF16), jax.ShapeDtypeStruct((m, D_FF), F32)],
        compiler_params=_params(1),
        name="ffn_up_dec",
    )(x, wq_up, wq_up, conv_w, conv_b, prev, prev)


def _layer(resid, xb, h_prev, layer, p, *, dec=None):
    m = xb.shape[0]
    t = _tiles(m)
    out = {}
    if dec is None:
        tm_in, tn_in = t['proj']
        h, out['wq_in'] = _proj(xb, p['w_in'], layer, n=IN_W, tm=tm_in, tn=tn_in,
                                stack=(DEPTH, h_prev), emit_bf16=True, name="in_proj")
    else:
        h = _proj_q(xb, dec['wq_in'][layer], tn=t['proj'][1], layer=layer, stack=(DEPTH, h_prev),
                    name="in_proj_dec")
    if dec is None:
        y_pool = _pool_seq(h, p['w_pool_grp'], p['pool_scale'], layer, tm=t['pool'])
        o_att = _attn_seq(h, layer)
    else:
        y_pool = _pool_dec(h, dec['pool'], p['w_pool_grp'], p['pool_scale'], layer)
        h4 = h.reshape(DEPTH, m, IN_W // HEAD_DIM, HEAD_DIM)
        o = _attn_dec(h4, dec['caches'], layer).reshape(DEC_BATCH, ATT_W)
        o_att = jnp.pad(o, ((0, m - DEC_BATCH), (0, 0))).astype(BF16)
    merged = _gate_merge(y_pool, o_att, h, p['w_br_pool'], p['w_br_att'], layer,
                         tm=t['gate'][0], tn=t['gate'][1])
    z = _proj(merged, p['w_out'], layer, n=D_MODEL, tm=t['proj_first'][0], tn=t['proj_first'][1],
              resid=resid, alpha=ALPHA, name="out_proj")
    x1b, mu1, rstd1 = _layer_norm(z, p['ln1_g'], p['ln1_b'], layer, tm=t['ln'])
    if dec is None:
        hm, tail, out['wq_up'] = _ffn_up_seq(x1b, p['w_up'], p['conv_w'], p['conv_b'], layer,
                                             tm=t['ffn_up'][0], tn=t['ffn_up'][1],
                                             n_chunks=t['ffn_chunks'])
    else:
        hm, tail = _ffn_up_dec(x1b, dec['wq_up'][layer], p['conv_w'], p['conv_b'], layer,
                               dec['conv'][layer], tn=t['ffn_up'][1])
    z2 = _down(hm, p['w_down'], layer, (z, mu1, rstd1, p['ln1_g'], p['ln1_b'], layer),
               tm=t['down'][0], tn=t['down'][1], tk=t['down'][2])
    if layer == DEPTH - 1:
        out['y'] = _layer_norm(z2, p['ln2_g'], p['ln2_b'], layer, tm=t['ln'], full=True)
    else:
        out['xb'], mu2, rstd2 = _layer_norm(z2, p['ln2_g'], p['ln2_b'], layer, tm=t['ln'])
        out['resid'] = (z2, mu2, rstd2, p['ln2_g'], p['ln2_b'], layer)
    out.update(h=h, tail=tail)
    return out


def _run_prompt(x_prompt, p):
    batch = x_prompt.shape[0]
    resid = x_prompt.reshape(batch * SEQ, D_MODEL)
    xb = resid.astype(BF16)
    h, tails, wq_in, wq_up = None, [], [], []
    for l in range(DEPTH):
        o = _layer(resid, xb, h, l, p)
        h, resid, xb = o['h'], o.get('resid'), o.get('xb')
        tails.append(o['tail'])
        wq_in.append(o['wq_in'])
        wq_up.append(o['wq_up'])
    x = o['y']
    h4 = h.reshape(DEPTH, batch, SEQ, IN_W)
    pool = h4[:, :, SEQ - POOL_CTX:, :POOL_W]
    kvs = []
    for g, (win, _) in enumerate(ATT_GROUPS):
        base = POOL_W + 3 * g * ATT_W + ATT_W
        keep = min(win, SEQ)
        kvs.append(h4[:, :, SEQ - keep:, base:base + 2 * ATT_W]
                   .reshape(DEPTH, batch, keep, 2, N_HEADS, HEAD_DIM))
    tm = _tiles(batch * SEQ)['ffn_up'][0]
    t5 = jnp.stack(tails).reshape(DEPTH, batch, SEQ // tm, SUBLANES, D_FF)
    conv = t5[:, :, SEQ // tm - 1, SUBLANES - 2:, :]
    return (x.reshape(batch, SEQ, D_MODEL), pool, *kvs, conv), wq_in, wq_up


def _run_sample(x_sample, state_pool, caches_in, state_conv, p, wq_in, wq_up):
    pad = DEC_ROWS - DEC_BATCH
    resid = jnp.pad(x_sample.reshape(DEC_BATCH, D_MODEL), ((0, pad), (0, 0)))
    dec = dict(
        wq_in=wq_in, wq_up=wq_up,
        pool=jnp.pad(jnp.transpose(state_pool, (0, 2, 1, 3)), ((0, 0), (0, 0), (0, pad), (0, 0))),
        conv=jnp.pad(jnp.transpose(state_conv, (0, 2, 1, 3)), ((0, 0), (0, 0), (0, pad), (0, 0))),
        caches=[c.reshape(DEPTH, DEC_BATCH, c.shape[2] // dil, dil, 2, N_HEADS, HEAD_DIM)
                for c, (_, dil) in zip(caches_in, ATT_GROUPS)],
    )
    xb = resid.astype(BF16)
    h, a_new = None, []
    for l in range(DEPTH):
        o = _layer(resid, xb, h, l, p, dec=dec)
        h, resid, xb = o['h'], o.get('resid'), o.get('xb')
        a_new.append(o['tail'][:DEC_BATCH])
    x = o['y']
    h = h[:, :DEC_BATCH]
    pool = jnp.concatenate([state_pool[:, :, 1:], h[:, :, None, :POOL_W]], axis=2)
    kvs = []
    for g, (win, _) in enumerate(ATT_GROUPS):
        base = POOL_W + 3 * g * ATT_W + ATT_W
        kv_new = h[:, :, base:base + 2 * ATT_W].reshape(DEPTH, DEC_BATCH, 1, 2, N_HEADS, HEAD_DIM)
        keep = min(win, PAST_LEN + 1)
        kvs.append(jnp.concatenate([caches_in[g], kv_new], axis=2)[:, :, caches_in[g].shape[2] + 1 - keep:])
    conv = jnp.concatenate([state_conv[:, :, 1:], jnp.stack(a_new)[:, :, None, :]], axis=2)
    return (x[:DEC_BATCH].reshape(DEC_BATCH, 1, D_MODEL), pool, *kvs, conv)


def kernel(x_prompt, x_sample, state_pool, cache_kv1, cache_kv2, cache_kv3, state_conv, w_in, w_pool_grp, pool_scale, w_br_pool, w_br_att, w_out, ln1_g, ln1_b, w_up, conv_w, conv_b, w_down, ln2_g, ln2_b):
    def row3(t):
        return t.reshape(DEPTH, 1, t.shape[-1])

    p = dict(w_in=w_in, w_pool_grp=w_pool_grp, pool_scale=row3(pool_scale), w_br_pool=w_br_pool,
             w_br_att=w_br_att, w_out=w_out, ln1_g=row3(ln1_g), ln1_b=row3(ln1_b), w_up=w_up,
             conv_w=conv_w, conv_b=row3(conv_b), w_down=w_down, ln2_g=row3(ln2_g), ln2_b=row3(ln2_b))
    (yp, pool_p, kv1_p, kv2_p, kv3_p, conv_p), wq_in, wq_up = _run_prompt(x_prompt, p)
    ys, pool_s, kv1_s, kv2_s, kv3_s, conv_s = _run_sample(
        x_sample, state_pool, (cache_kv1, cache_kv2, cache_kv3), state_conv, p, wq_in, wq_up)
    return (yp, ys, pool_p, pool_s, kv1_p, kv1_s, kv2_p, kv2_s, kv3_p, kv3_s, conv_p, conv_s)
```
